```python
import math
import jax, jax.numpy as jnp
from jax import lax
import numpy as np

D_MODEL = 2048
BATCH = 8
SEQ = 2048
DEPTH = 2
DEC_BATCH = 16
DEC_SEQ = 16
PAST_LEN = 1024

CHUNK = 64
D_POOL = D_MODEL
POOL_WINDOWS = (2, 4, 8, 16)
N_POOL_GROUPS = len(POOL_WINDOWS)
POOL_GROUP = D_POOL // N_POOL_GROUPS
POOL_KEEP = max(POOL_WINDOWS) - 1
D_INNER = 2 * D_MODEL
SSD_HEAD_DIM = 64
SSD_HEADS = D_INNER // SSD_HEAD_DIM
SSD_GROUPS = 8
HEADS_PER_GROUP = SSD_HEADS // SSD_GROUPS
D_STATE = 128
CONV_W = 4
D_CONV = D_INNER + 2 * SSD_GROUPS * D_STATE
SSD_CHUNK = CHUNK
NORM_GROUP = D_INNER // SSD_GROUPS
D_FF = 4 * D_MODEL
N_BRANCH = 2
OFF_POOL = 0
OFF_Z = OFF_POOL + D_POOL
OFF_XBC = OFF_Z + D_INNER
OFF_DT = OFF_XBC + D_CONV
OFF_GATE = OFF_DT + SSD_HEADS
IN_COLS = OFF_GATE + N_BRANCH * D_MODEL
EPS = 1e-6

kernel_name = 'hybrid_pool_ssd_stream_step'


def rmsnorm(x, g):
    xf = x.astype(jnp.float32)
    y = xf * lax.rsqrt(jnp.mean(xf * xf, axis=-1, keepdims=True) + EPS)
    return (y * g.astype(jnp.float32)).astype(x.dtype)


def gated_rmsnorm(y, z, g):
    h = y.astype(jnp.float32) * jax.nn.silu(z.astype(jnp.float32))
    lead = h.shape[:-1]
    h = h.reshape(lead + (SSD_GROUPS, NORM_GROUP))
    h = h * lax.rsqrt(jnp.mean(h * h, axis=-1, keepdims=True) + EPS)
    return h.reshape(lead + (D_INNER,)) * g.astype(jnp.float32)


def pool_mixer(p, st, pos0, pool_w, pool_scale):
    L = p.shape[1]
    xp = jnp.concatenate([st.astype(p.dtype), p], axis=1)
    s = jnp.cumsum(xp.astype(jnp.float32), axis=1)
    s = jnp.pad(s, ((0, 0), (1, 0), (0, 0)))
    pos = pos0 + jnp.arange(L)
    outs = []
    for g, w in enumerate(POOL_WINDOWS):
        lo, hi = g * POOL_GROUP, (g + 1) * POOL_GROUP
        win = s[:, POOL_KEEP + 1:POOL_KEEP + 1 + L, lo:hi] - s[:, POOL_KEEP + 1 - w:POOL_KEEP + 1 - w + L, lo:hi]
        cnt = jnp.minimum(w, pos + 1).astype(jnp.float32)
        d = win / cnt[None, :, None] - p[:, :, lo:hi].astype(jnp.float32)
        outs.append(jnp.einsum('blc,cd->bld', d.astype(p.dtype), pool_w[g]))
    out = jnp.concatenate(outs, axis=-1) * pool_scale
    return out, xp[:, -POOL_KEEP:]


def causal_conv(xbc, st, w, b):
    xc = jnp.concatenate([st.astype(xbc.dtype), xbc], axis=1)
    y = lax.conv_general_dilated(xc, w[:, None, :], window_strides=(1,), padding='VALID',
                                 dimension_numbers=('NWC', 'WIO', 'NWC'), feature_group_count=D_CONV)
    return jax.nn.silu(y + b), xc[:, -(CONV_W - 1):]


def ssd_scan(xh, dt, A, Bm, Cm, h0):
    b, L = xh.shape[:2]
    pad = (-L) % SSD_CHUNK
    nc = (L + pad) // SSD_CHUNK

    def to_chunks(a):
        a = jnp.pad(a, [(0, 0), (0, pad)] + [(0, 0)] * (a.ndim - 2))
        a = a.reshape((b, nc, SSD_CHUNK) + a.shape[2:])
        return jnp.moveaxis(a, 1, 0)

    causal = jnp.tril(jnp.ones((SSD_CHUNK, SSD_CHUNK), dtype=bool))[None, :, :, None, None]

    def step(h, inp):
        xc, dtc, Bc, Cc = inp
        cum = jnp.cumsum(dtc * A, axis=1)
        seg = cum[:, :, None] - cum[:, None, :]
        decay = jnp.exp(jnp.where(causal, seg, -jnp.inf))
        cb = jnp.einsum('btgn,bsgn->btsg', Cc, Bc)
        xdt = xc * dtc[..., None]
        y = jnp.einsum('btsgr,bsgrp->btgrp', cb[..., None] * decay, xdt)
        y = y + jnp.einsum('btgn,bgrpn->btgrp', Cc, h) * jnp.exp(cum)[..., None]
        last = cum[:, -1]
        w_end = jnp.exp(last[:, None] - cum)
        h = h * jnp.exp(last)[..., None, None] + jnp.einsum('bsgn,bsgr,bsgrp->bgrpn', Bc, w_end, xdt)
        return h, y

    h, ys = lax.scan(step, h0, (to_chunks(xh), to_chunks(dt), to_chunks(Bm), to_chunks(Cm)))
    ys = jnp.moveaxis(ys, 0, 1).reshape((b, nc * SSD_CHUNK) + xh.shape[2:])
    return ys[:, :L], h


def mixer(u, st_pool, st_conv, st_ssm, pos0, p):
    b, L, _ = u.shape
    proj = u @ p['w_in']
    pool_in = proj[..., OFF_POOL:OFF_Z]
    z = proj[..., OFF_Z:OFF_XBC]
    xbc = proj[..., OFF_XBC:OFF_DT]
    dt_raw = proj[..., OFF_DT:OFF_GATE]
    gates = jax.nn.sigmoid(proj[..., OFF_GATE:].astype(jnp.float32)).reshape(b, L, N_BRANCH, D_MODEL)

    a_pool, new_pool = pool_mixer(pool_in, st_pool, pos0, p['pool_w'], p['pool_scale'])

    xbc, new_conv = causal_conv(xbc, st_conv, p['conv_w'], p['conv_b'])
    xbc = xbc.astype(jnp.float32)
    xh = xbc[..., :D_INNER].reshape(b, L, SSD_GROUPS, HEADS_PER_GROUP, SSD_HEAD_DIM)
    Bm = xbc[..., D_INNER:D_INNER + SSD_GROUPS * D_STATE].reshape(b, L, SSD_GROUPS, D_STATE)
    Cm = xbc[..., D_INNER + SSD_GROUPS * D_STATE:].reshape(b, L, SSD_GROUPS, D_STATE)
    dt = jax.nn.softplus(dt_raw.astype(jnp.float32) + p['dt_bias'].astype(jnp.float32))
    dt = dt.reshape(b, L, SSD_GROUPS, HEADS_PER_GROUP)
    A = -jnp.exp(p['a_log'].astype(jnp.float32)).reshape(SSD_GROUPS, HEADS_PER_GROUP)
    h0 = st_ssm.astype(jnp.float32).reshape(b, SSD_GROUPS, HEADS_PER_GROUP, SSD_HEAD_DIM, D_STATE)
    y, h = ssd_scan(xh, dt, A, Bm, Cm, h0)
    y = y + p['d_skip'].astype(jnp.float32).reshape(SSD_GROUPS, HEADS_PER_GROUP)[..., None] * xh
    a_ssd = gated_rmsnorm(y.reshape(b, L, D_INNER), z, p['ssd_norm']).astype(u.dtype)

    br_pool = (a_pool @ p['w_pool_proj']).astype(jnp.float32)
    br_ssd = (a_ssd @ p['w_ssd_proj']).astype(jnp.float32)
    merged = (gates[:, :, 0] * br_pool + gates[:, :, 1] * br_ssd).astype(u.dtype)
    out = merged @ p['w_out']
    new_ssm = h.reshape(b, SSD_HEADS, SSD_HEAD_DIM, D_STATE).astype(st_ssm.dtype)
    return out, new_pool, new_conv, new_ssm


def layer(x, st_pool, st_conv, st_ssm, pos0, p):
    m, new_pool, new_conv, new_ssm = mixer(rmsnorm(x, p['g_mix_pre']), st_pool, st_conv, st_ssm, pos0, p)
    x = x + rmsnorm(m, p['g_mix_post'])
    hdn = jnp.square(jax.nn.relu(rmsnorm(x, p['g_mlp_pre']) @ p['w_up'])) @ p['w_down']
    x = x + rmsnorm(hdn, p['g_mlp_post'])
    return x, new_pool, new_conv, new_ssm


def setup_inputs(seed: int = 0) -> dict:
    key = jax.random.key(seed)
    ks = jax.random.split(key, 24)
    f32 = jnp.float32
    nrm = lambda k, shape, s: jax.random.normal(k, shape, f32) * s
    dt0 = jnp.exp(jax.random.uniform(ks[9], (DEPTH, SSD_HEADS), f32, math.log(1e-3), math.log(1e-1)))
    return {
        'x_prompt': nrm(ks[0], (BATCH, SEQ, D_MODEL), 1.0),
        'x_sample': nrm(ks[1], (DEC_BATCH, DEC_SEQ, D_MODEL), 1.0),
        'state_pool': nrm(ks[2], (DEPTH, DEC_BATCH, POOL_KEEP, D_POOL), 1.0),
        'state_conv': nrm(ks[3], (DEPTH, DEC_BATCH, CONV_W - 1, D_CONV), 1.0),
        'state_ssm': nrm(ks[4], (DEPTH, DEC_BATCH, SSD_HEADS, SSD_HEAD_DIM, D_STATE), 0.1),
        'w_in': nrm(ks[5], (DEPTH, D_MODEL, IN_COLS), D_MODEL ** -0.5),
        'pool_w': nrm(ks[6], (DEPTH, N_POOL_GROUPS, POOL_GROUP, POOL_GROUP), POOL_GROUP ** -0.5),
        'pool_scale': 1.0 + nrm(ks[7], (DEPTH, D_POOL), 0.1),
        'conv_w': nrm(ks[8], (DEPTH, CONV_W, D_CONV), CONV_W ** -0.5),
        'conv_b': nrm(ks[10], (DEPTH, D_CONV), 0.01),
        'dt_bias': dt0 + jnp.log(-jnp.expm1(-dt0)),
        'a_log': jnp.log(jax.random.uniform(ks[11], (DEPTH, SSD_HEADS), f32, 1.0, 16.0)),
        'd_skip': 1.0 + nrm(ks[12], (DEPTH, SSD_HEADS), 0.1),
        'ssd_norm': 1.0 + nrm(ks[13], (DEPTH, D_INNER), 0.1),
        'w_pool_proj': nrm(ks[14], (DEPTH, D_POOL, D_MODEL), D_POOL ** -0.5),
        'w_ssd_proj': nrm(ks[15], (DEPTH, D_INNER, D_MODEL), D_INNER ** -0.5),
        'w_out': nrm(ks[16], (DEPTH, D_MODEL, D_MODEL), D_MODEL ** -0.5),
        'g_mix_pre': 1.0 + nrm(ks[17], (DEPTH, D_MODEL), 0.1),
        'g_mix_post': 1.0 + nrm(ks[18], (DEPTH, D_MODEL), 0.1),
        'w_up': nrm(ks[19], (DEPTH, D_MODEL, D_FF), D_MODEL ** -0.5),
        'w_down': nrm(ks[20], (DEPTH, D_FF, D_MODEL), D_FF ** -0.5),
        'g_mlp_pre': 1.0 + nrm(ks[21], (DEPTH, D_MODEL), 0.1),
        'g_mlp_post': 1.0 + nrm(ks[22], (DEPTH, D_MODEL), 0.1),
    }


def reference(x_prompt, x_sample, state_pool, state_conv, state_ssm, w_in, pool_w, pool_scale, conv_w, conv_b,
              dt_bias, a_log, d_skip, ssd_norm, w_pool_proj, w_ssd_proj, w_out, g_mix_pre, g_mix_post,
              w_up, w_down, g_mlp_pre, g_mlp_post):
    bp = x_prompt.shape[0]
    zero_pool = jnp.zeros((bp, POOL_KEEP, D_POOL), x_prompt.dtype)
    zero_conv = jnp.zeros((bp, CONV_W - 1, D_CONV), x_prompt.dtype)
    zero_ssm = jnp.zeros((bp, SSD_HEADS, SSD_HEAD_DIM, D_STATE), state_ssm.dtype)
    yp, ys = x_prompt, x_sample
    pool_p, conv_p, ssm_p, pool_s, conv_s, ssm_s = [], [], [], [], [], []
    for l in range(DEPTH):
        p = {'w_in': w_in[l], 'pool_w': pool_w[l], 'pool_scale': pool_scale[l], 'conv_w': conv_w[l],
             'conv_b': conv_b[l], 'dt_bias': dt_bias[l], 'a_log': a_log[l], 'd_skip': d_skip[l],
             'ssd_norm': ssd_norm[l], 'w_pool_proj': w_pool_proj[l], 'w_ssd_proj': w_ssd_proj[l],
             'w_out': w_out[l], 'g_mix_pre': g_mix_pre[l], 'g_mix_post': g_mix_post[l], 'w_up': w_up[l],
             'w_down': w_down[l], 'g_mlp_pre': g_mlp_pre[l], 'g_mlp_post': g_mlp_post[l]}
        yp, sp, sc, ss = layer(yp, zero_pool, zero_conv, zero_ssm, 0, p)
        pool_p.append(sp); conv_p.append(sc); ssm_p.append(ss)
        ys, sp, sc, ss = layer(ys, state_pool[l], state_conv[l], state_ssm[l], PAST_LEN, p)
        pool_s.append(sp); conv_s.append(sc); ssm_s.append(ss)
    return (yp, ys, jnp.stack(pool_p), jnp.stack(conv_p), jnp.stack(ssm_p),
            jnp.stack(pool_s), jnp.stack(conv_s), jnp.stack(ssm_s))
```

```python
import functools

import jax
import jax.numpy as jnp
from jax import lax
from jax.experimental import pallas as pl
from jax.experimental.pallas import tpu as pltpu

F32 = jnp.float32
BF16 = jnp.bfloat16

D_MODEL = 2048
EPS = 1e-6
POOL_WINDOWS = (2, 4, 8, 16)
POOL_GROUP = D_MODEL // len(POOL_WINDOWS)
POOL_KEEP = max(POOL_WINDOWS) - 1
POOL_HALO = 16
D_INNER = 2 * D_MODEL
HEAD_DIM = 64
N_HEADS = D_INNER // HEAD_DIM
N_GROUPS = 8
HEADS_PER_GROUP = N_HEADS // N_GROUPS
GROUP_W = D_INNER // N_GROUPS
D_STATE = 128
CONV_W = 4
D_CONV = D_INNER + 2 * N_GROUPS * D_STATE
CONV_HALO = 8
SSD_Q = 128
D_FF = 4 * D_MODEL
COL_XBC = 0
COL_POOL = COL_XBC + D_CONV
COL_Z = COL_POOL + D_MODEL
COL_G0 = COL_Z + D_INNER
COL_G1 = COL_G0 + D_MODEL
P_COLS = COL_G1 + D_MODEL
DT_LANES = 128
NEG_BIG = -1e30
LANE = 128
VMEM_LIMIT = 56 * 1024 * 1024


def _params(n_axes):
    return pltpu.CompilerParams(dimension_semantics=("arbitrary",) * n_axes, vmem_limit_bytes=VMEM_LIMIT)


def _sigmoid(v):
    return 1.0 / (1.0 + jnp.exp(-v))


def _rms_scale(v, g):
    ms = jnp.mean(v * v, axis=-1, keepdims=True)
    return v * lax.rsqrt(ms + EPS) * g


def _dot(a, b):
    return jnp.dot(a, b, preferred_element_type=F32)


def _in_proj_kernel(x_ref, g_ref, w_ref, wdt_ref, dtb_ref, p_ref, dt_ref, u_ref, *, tn):
    j = pl.program_id(1)

    @pl.when(j == 0)
    def _():
        u = _rms_scale(x_ref[...], g_ref[...]).astype(BF16)
        u_ref[...] = u
        raw = _dot(u, wdt_ref[...]) + dtb_ref[...]
        dt_ref[...] = jnp.maximum(raw, 0.0) + jnp.log1p(jnp.exp(-jnp.abs(raw)))

    acc = _dot(u_ref[...], w_ref[...])
    col = j * tn

    @pl.when(col < COL_Z)
    def _():
        p_ref[...] = acc.astype(BF16)

    @pl.when(jnp.logical_and(col >= COL_Z, col < COL_G0))
    def _():
        p_ref[...] = (acc * _sigmoid(acc)).astype(BF16)

    @pl.when(col >= COL_G0)
    def _():
        p_ref[...] = _sigmoid(acc).astype(BF16)


def _in_proj(x, g, w, wdt, dtb, *, tm, tn):
    m = x.shape[0]
    return pl.pallas_call(
        functools.partial(_in_proj_kernel, tn=tn),
        grid=(m // tm, P_COLS // tn),
        in_specs=[
            pl.BlockSpec((tm, D_MODEL), lambda i, j: (i, 0)),
            pl.BlockSpec((1, D_MODEL), lambda i, j: (0, 0)),
            pl.BlockSpec((D_MODEL, tn), lambda i, j: (0, j)),
            pl.BlockSpec((D_MODEL, DT_LANES), lambda i, j: (0, 0)),
            pl.BlockSpec((1, DT_LANES), lambda i, j: (0, 0)),
        ],
        out_specs=[
            pl.BlockSpec((tm, tn), lambda i, j: (i, j)),
            pl.BlockSpec((tm, DT_LANES), lambda i, j: (i, 0)),
        ],
        out_shape=[jax.ShapeDtypeStruct((m, P_COLS), BF16), jax.ShapeDtypeStruct((m, DT_LANES), F32)],
        scratch_shapes=[pltpu.VMEM((tm, D_MODEL), BF16)],
        compiler_params=_params(2),
        name="in_proj",
    )(x, g, w, wdt, dtb)


def _pool_kernel(p_ref, g0_ref, sth_ref, stl_ref, pw_ref, ps_ref, wpp_ref, o_ref, hh_ref, hl_ref, a_ref,
                 *, lt, pos0):
    ti = pl.program_id(1)

    @pl.when(ti == 0)
    def _():
        hh_ref[...] = sth_ref[0]
        hl_ref[...] = stl_ref[0]

    cur = p_ref[...]
    diff = lax.broadcasted_iota(jnp.int32, (lt, lt), 0) - lax.broadcasted_iota(jnp.int32, (lt, lt), 1)
    dist = (lax.broadcasted_iota(jnp.int32, (lt, POOL_HALO), 0) + POOL_HALO
            - lax.broadcasted_iota(jnp.int32, (lt, POOL_HALO), 1))
    pos = pos0 + ti * lt + lax.broadcasted_iota(jnp.int32, (lt, 1), 0)
    for g, w in enumerate(POOL_WINDOWS):
        cs = slice(g * POOL_GROUP, (g + 1) * POOL_GROUP)
        band = jnp.where(diff >= 0, jnp.where(diff < w, 1.0, 0.0), 0.0).astype(BF16)
        hband = jnp.where(dist < w, 1.0, 0.0).astype(BF16)
        xg = cur[:, cs]
        win = _dot(band, xg) + _dot(hband, hh_ref[:, cs]) + _dot(hband, hl_ref[:, cs])
        cnt = jnp.minimum(w, pos + 1).astype(F32)
        d = win * (1.0 / cnt) - xg.astype(F32)
        a_ref[:, cs] = (_dot(d.astype(BF16), pw_ref[g]) * ps_ref[:, cs]).astype(BF16)
    o_ref[...] = (_dot(a_ref[...], wpp_ref[...]) * g0_ref[...].astype(F32)).astype(BF16)
    hh_ref[...] = cur[lt - POOL_HALO:, :]
    hl_ref[...] = jnp.zeros_like(hl_ref)


def _pool_branch(pall, st_hi, st_lo, pool_w, pool_scale, wpp, *, b, l, lt, pos0):
    nt = l // lt
    m = b * l
    return pl.pallas_call(
        functools.partial(_pool_kernel, lt=lt, pos0=pos0),
        grid=(b, nt),
        in_specs=[
            pl.BlockSpec((lt, D_MODEL), lambda bi, ti: (bi * nt + ti, COL_POOL // D_MODEL)),
            pl.BlockSpec((lt, D_MODEL), lambda bi, ti: (bi * nt + ti, COL_G0 // D_MODEL)),
            pl.BlockSpec((1, POOL_HALO, D_MODEL), lambda bi, ti: (bi, 0, 0)),
            pl.BlockSpec((1, POOL_HALO, D_MODEL), lambda bi, ti: (bi, 0, 0)),
            pl.BlockSpec((len(POOL_WINDOWS), POOL_GROUP, POOL_GROUP), lambda bi, ti: (0, 0, 0)),
            pl.BlockSpec((1, D_MODEL), lambda bi, ti: (0, 0)),
            pl.BlockSpec((D_MODEL, D_MODEL), lambda bi, ti: (0, 0)),
        ],
        out_specs=pl.BlockSpec((lt, D_MODEL), lambda bi, ti: (bi * nt + ti, 0)),
        out_shape=jax.ShapeDtypeStruct((m, D_MODEL), BF16),
        scratch_shapes=[
            pltpu.VMEM((POOL_HALO, D_MODEL), BF16),
            pltpu.VMEM((POOL_HALO, D_MODEL), BF16),
            pltpu.VMEM((lt, D_MODEL), BF16),
        ],
        compiler_params=_params(2),
        name="pool_branch",
    )(pall, pall, st_hi, st_lo, pool_w, pool_scale, wpp)


def _split3(v):
    hi = v.astype(BF16)
    r1 = v - hi.astype(F32)
    mid = r1.astype(BF16)
    lo = (r1 - mid.astype(F32)).astype(BF16)
    return hi, mid, lo


def _ssd_kernel(xbc_ref, zs_ref, dtt_ref, stc_ref, sts_ref, cw_ref, cb_ref, alog_ref, dsk_ref, nw_ref,
                o_ref, ns_ref, xc_ref, xa_ref, ht_ref, dtp_ref, cum_ref, *, lv, nt):
    q = SSD_Q
    ti = pl.program_id(1)

    @pl.when(ti == 0)
    def _():
        xc_ref[0:CONV_HALO, :] = stc_ref[0]
        for g in range(N_GROUPS):
            st = sts_ref[0, g * HEADS_PER_GROUP:(g + 1) * HEADS_PER_GROUP]
            ht_ref[g] = st.reshape(GROUP_W, D_STATE).T

    if lv < q:
        xa_ref[lv:q, :] = jnp.zeros((q - lv, D_CONV), F32)
    cwid = 512

    def conv_body(c, carry):
        cs = pl.ds(pl.multiple_of(c * cwid, cwid), cwid)
        xc_ref[CONV_HALO:CONV_HALO + lv, cs] = xbc_ref[:, cs].astype(F32)
        acc = xc_ref[CONV_HALO:CONV_HALO + lv, cs] * cw_ref[CONV_W - 1:CONV_W, cs] + cb_ref[:, cs]
        for k in range(CONV_W - 1):
            r0 = CONV_HALO - (CONV_W - 1) + k
            acc = acc + xc_ref[r0:r0 + lv, cs] * cw_ref[k:k + 1, cs]
        xa_ref[0:lv, cs] = acc * _sigmoid(acc)
        return carry

    lax.fori_loop(0, D_CONV // cwid, conv_body, 0)
    xc_ref[0:CONV_HALO, :] = xc_ref[lv:lv + CONV_HALO, :]

    if lv < q:
        dtp_ref[...] = jnp.zeros((DT_LANES, q), F32)
        dtp_ref[:, 0:lv] = dtt_ref[0]
    else:
        dtp_ref[...] = dtt_ref[0]
    a_t = dtp_ref[...] * (-jnp.exp(alog_ref[...]))
    row_i = lax.broadcasted_iota(jnp.int32, (q, q), 0)
    col_i = lax.broadcasted_iota(jnp.int32, (q, q), 1)
    upper = jnp.where(row_i <= col_i, 1.0, 0.0).astype(BF16)
    hi, mid, lo = _split3(a_t)
    cum_ref[...] = _dot(hi, upper) + _dot(mid, upper) + _dot(lo, upper)
    causal = row_i >= col_i

    def group_body(g, carry):
        go = g * GROUP_W
        bg = xa_ref[:, pl.ds(pl.multiple_of(D_INNER + g * D_STATE, D_STATE), D_STATE)]
        cg = xa_ref[:, pl.ds(pl.multiple_of(D_INNER + N_GROUPS * D_STATE + g * D_STATE, D_STATE), D_STATE)]
        cg16 = cg.astype(BF16)
        cb = lax.dot_general(cg16, bg.astype(BF16), (((1,), (1,)), ((), ())), preferred_element_type=F32)
        bt = bg.T
        hg = ht_ref[g]
        y_state = _dot(cg16, hg.astype(BF16))
        ys = []
        hs = []
        for r in range(HEADS_PER_GROUP):
            h = g * HEADS_PER_GROUP + r
            cum_row = cum_ref[pl.ds(h, 1), :]
            dt_row = dtp_ref[pl.ds(h, 1), :]
            cum_s = jnp.broadcast_to(cum_row, (q, q))
            cum_t = cum_s.T
            decay = jnp.exp(jnp.where(causal, cum_t - cum_s, NEG_BIG))
            mix = (cb * decay * dt_row).astype(BF16)
            lo_off = pl.multiple_of(go + (r // 2) * LANE, LANE)
            hsel = slice((r % 2) * HEAD_DIM, (r % 2 + 1) * HEAD_DIM)
            xr = xa_ref[:, pl.ds(lo_off, LANE)][:, hsel]
            xr16 = xr.astype(BF16)
            dsk = dsk_ref[:, pl.ds(lo_off, LANE)][:, hsel]
            hd = slice(r * HEAD_DIM, (r + 1) * HEAD_DIM)
            y = _dot(mix, xr16) + y_state[:, hd] * jnp.exp(cum_t[:, 0:HEAD_DIM]) + dsk * xr
            ys.append(y)
            last = cum_row[:, q - 1:q]
            w_row = jnp.exp(last - cum_row) * dt_row
            upd = _dot((bt * w_row).astype(BF16), xr16)
            hs.append(hg[:, hd] * jnp.exp(last) + upd)
        ht_ref[g] = jnp.concatenate(hs, axis=1)
        y_all = jnp.concatenate(ys, axis=1)
        gs = pl.ds(pl.multiple_of(go, GROUP_W), GROUP_W)
        gated = y_all[0:lv] * zs_ref[:, gs].astype(F32)
        o_ref[:, gs] = _rms_scale(gated, nw_ref[:, gs]).astype(BF16)
        return carry

    lax.fori_loop(0, N_GROUPS, group_body, 0)

    @pl.when(ti == nt - 1)
    def _():
        for g in range(N_GROUPS):
            ns_ref[0, g * HEADS_PER_GROUP:(g + 1) * HEADS_PER_GROUP] = (
                ht_ref[g].T.reshape(HEADS_PER_GROUP, HEAD_DIM, D_STATE))


def _ssd_branch(pall, dtt, st_conv, st_ssm, conv_w, conv_b, alog_col, dskip_row, norm_w, *, b, l):
    lv = min(l, SSD_Q)
    nt = l // lv
    m = b * l
    return pl.pallas_call(
        functools.partial(_ssd_kernel, lv=lv, nt=nt),
        grid=(b, nt),
        in_specs=[
            pl.BlockSpec((lv, D_CONV), lambda bi, ti: (bi * nt + ti, COL_XBC // D_CONV)),
            pl.BlockSpec((lv, D_INNER), lambda bi, ti: (bi * nt + ti, COL_Z // D_INNER)),
            pl.BlockSpec((1, DT_LANES, lv), lambda bi, ti: (bi, 0, ti)),
            pl.BlockSpec((1, CONV_HALO, D_CONV), lambda bi, ti: (bi, 0, 0)),
            pl.BlockSpec((1, N_HEADS, HEAD_DIM, D_STATE), lambda bi, ti: (bi, 0, 0, 0)),
            pl.BlockSpec((CONV_W, D_CONV), lambda bi, ti: (0, 0)),
            pl.BlockSpec((1, D_CONV), lambda bi, ti: (0, 0)),
            pl.BlockSpec((DT_LANES, 1), lambda bi, ti: (0, 0)),
            pl.BlockSpec((1, D_INNER), lambda bi, ti: (0, 0)),
            pl.BlockSpec((1, D_INNER), lambda bi, ti: (0, 0)),
        ],
        out_specs=[
            pl.BlockSpec((lv, D_INNER), lambda bi, ti: (bi * nt + ti, 0)),
            pl.BlockSpec((1, N_HEADS, HEAD_DIM, D_STATE), lambda bi, ti: (bi, 0, 0, 0)),
        ],
        out_shape=[
            jax.ShapeDtypeStruct((m, D_INNER), BF16),
            jax.ShapeDtypeStruct((b, N_HEADS, HEAD_DIM, D_STATE), F32),
        ],
        scratch_shapes=[
            pltpu.VMEM((CONV_HALO + SSD_Q, D_CONV), F32),
            pltpu.VMEM((SSD_Q, D_CONV), F32),
            pltpu.VMEM((N_GROUPS, D_STATE, GROUP_W), F32),
            pltpu.VMEM((DT_LANES, SSD_Q), F32),
            pltpu.VMEM((DT_LANES, SSD_Q), F32),
        ],
        compiler_params=_params(2),
        name="ssd_branch",
    )(pall, pall, dtt, st_conv, st_ssm, conv_w, conv_b, alog_col, dskip_row, norm_w)


def _mix_out_kernel(a_ref, ws_ref, bp_ref, g1_ref, wo_ref, gp_ref, x_ref, o_ref, acc_ref, *, nk):
    k = pl.program_id(1)

    @pl.when(k == 0)
    def _():
        acc_ref[...] = jnp.zeros_like(acc_ref)

    acc_ref[...] += _dot(a_ref[...], ws_ref[...])

    @pl.when(k == nk - 1)
    def _():
        merged = (bp_ref[...].astype(F32) + g1_ref[...].astype(F32) * acc_ref[...]).astype(BF16)
        out = _dot(merged, wo_ref[...])
        o_ref[...] = x_ref[...] + _rms_scale(out, gp_ref[...])


def _mix_out(a_ssd, w_ssd, bp, pall, w_out, g_post, x, *, tm, tk):
    m = x.shape[0]
    nk = D_INNER // tk
    return pl.pallas_call(
        functools.partial(_mix_out_kernel, nk=nk),
        grid=(m // tm, nk),
        in_specs=[
            pl.BlockSpec((tm, tk), lambda i, k: (i, k)),
            pl.BlockSpec((tk, D_MODEL), lambda i, k: (k, 0)),
            pl.BlockSpec((tm, D_MODEL), lambda i, k: (i, 0)),
            pl.BlockSpec((tm, D_MODEL), lambda i, k: (i, COL_G1 // D_MODEL)),
            pl.BlockSpec((D_MODEL, D_MODEL), lambda i, k: (0, 0)),
            pl.BlockSpec((1, D_MODEL), lambda i, k: (0, 0)),
            pl.BlockSpec((tm, D_MODEL), lambda i, k: (i, 0)),
        ],
        out_specs=pl.BlockSpec((tm, D_MODEL), lambda i, k: (i, 0)),
        out_shape=jax.ShapeDtypeStruct((m, D_MODEL), F32),
        scratch_shapes=[pltpu.VMEM((tm, D_MODEL), F32)],
        compiler_params=_params(2),
        name="mix_out",
    )(a_ssd, w_ssd, bp, pall, w_out, g_post, x)


def _mlp_kernel(x_ref, gpre_ref, wu_ref, wd_ref, gpost_ref, o_ref, u_ref, acc_ref, *, nf):
    j = pl.program_id(1)

    @pl.when(j == 0)
    def _():
        u_ref[...] = _rms_scale(x_ref[...], gpre_ref[...]).astype(BF16)
        acc_ref[...] = jnp.zeros_like(acc_ref)

    hid = jnp.maximum(_dot(u_ref[...], wu_ref[...]), 0.0)
    acc_ref[...] += _dot((hid * hid).astype(BF16), wd_ref[...])

    @pl.when(j == nf - 1)
    def _():
        o_ref[...] = x_ref[...] + _rms_scale(acc_ref[...], gpost_ref[...])


def _mlp(x, g_pre, w_up, w_down, g_post, *, tm, tf):
    m = x.shape[0]
    nf = D_FF // tf
    return pl.pallas_call(
        functools.partial(_mlp_kernel, nf=nf),
        grid=(m // tm, nf),
        in_specs=[
            pl.BlockSpec((tm, D_MODEL), lambda i, j: (i, 0)),
            pl.BlockSpec((1, D_MODEL), lambda i, j: (0, 0)),
            pl.BlockSpec((D_MODEL, tf), lambda i, j: (0, j)),
            pl.BlockSpec((tf, D_MODEL), lambda i, j: (j, 0)),
            pl.BlockSpec((1, D_MODEL), lambda i, j: (0, 0)),
        ],
        out_specs=pl.BlockSpec((tm, D_MODEL), lambda i, j: (i, 0)),
        out_shape=jax.ShapeDtypeStruct((m, D_MODEL), F32),
        scratch_shapes=[pltpu.VMEM((tm, D_MODEL), BF16), pltpu.VMEM((tm, D_MODEL), F32)],
        compiler_params=_params(2),
        name="mlp",
    )(x, g_pre, w_up, w_down, g_post)


def _tiles(m):
    return min(m, 512)


def _layer_group(x, st_pool, st_conv, st_ssm, pos0, p, *, b, l):
    m = b * l
    tm = _tiles(m)
    pall, dt = _in_proj(x, p["g_mix_pre"], p["w_in"], p["w_dt"], p["dt_bias"], tm=tm, tn=1024)

    st16 = jnp.pad(st_pool, ((0, 0), (POOL_HALO - POOL_KEEP, 0), (0, 0)))
    st_hi = st16.astype(BF16)
    st_lo = (st16 - st_hi.astype(F32)).astype(BF16)
    bp = _pool_branch(pall, st_hi, st_lo, p["pool_w"], p["pool_scale"], p["w_pool_proj"],
                      b=b, l=l, lt=min(l, 256), pos0=pos0)

    dtt = jnp.swapaxes(dt.reshape(b, l, DT_LANES), 1, 2)
    stc = jnp.pad(st_conv, ((0, 0), (CONV_HALO - (CONV_W - 1), 0), (0, 0)))
    a_ssd, new_ssm = _ssd_branch(pall, dtt, stc, st_ssm, p["conv_w"], p["conv_b"], p["a_log"], p["d_skip"],
                                 p["ssd_norm"], b=b, l=l)

    x = _mix_out(a_ssd, p["w_ssd_proj"], bp, pall, p["w_out"], p["g_mix_post"], x, tm=min(m, 256), tk=1024)
    x = _mlp(x, p["g_mlp_pre"], p["w_up"], p["w_down"], p["g_mlp_post"], tm=tm, tf=1024)

    p3 = pall.reshape(b, l, P_COLS)
    new_pool = p3[:, l - POOL_KEEP:, COL_POOL:COL_POOL + D_MODEL].astype(F32)
    new_conv = p3[:, l - (CONV_W - 1):, COL_XBC:COL_XBC + D_CONV].astype(F32)
    return x, new_pool, new_conv, new_ssm


def _prep_layer(w_in, pool_w, pool_scale, conv_w, conv_b, dt_bias, a_log, d_skip, ssd_norm, w_pool_proj,
                w_ssd_proj, w_out, g_mix_pre, g_mix_post, w_up, w_down, g_mlp_pre, g_mlp_post):
    o_z = D_MODEL
    o_xbc = o_z + D_INNER
    o_dt = o_xbc + D_CONV
    o_gate = o_dt + N_HEADS
    w_main = jnp.concatenate([w_in[:, o_xbc:o_dt], w_in[:, 0:o_z], w_in[:, o_z:o_xbc], w_in[:, o_gate:]],
                             axis=1).astype(BF16)
    w_dt = jnp.pad(w_in[:, o_dt:o_gate], ((0, 0), (0, DT_LANES - N_HEADS))).astype(BF16)
    row = lambda v: v.reshape(1, -1)
    return {
        "w_in": w_main,
        "w_dt": w_dt,
        "dt_bias": row(jnp.pad(dt_bias, (0, DT_LANES - N_HEADS))),
        "pool_w": pool_w.astype(BF16),
        "pool_scale": row(pool_scale),
        "conv_w": conv_w,
        "conv_b": row(conv_b),
        "a_log": jnp.pad(a_log, (0, DT_LANES - N_HEADS)).reshape(DT_LANES, 1),
        "d_skip": row(jnp.repeat(d_skip, HEAD_DIM)),
        "ssd_norm": row(ssd_norm),
        "w_pool_proj": w_pool_proj.astype(BF16),
        "w_ssd_proj": w_ssd_proj.astype(BF16),
        "w_out": w_out.astype(BF16),
        "g_mix_pre": row(g_mix_pre),
        "g_mix_post": row(g_mix_post),
        "w_up": w_up.astype(BF16),
        "w_down": w_down.astype(BF16),
        "g_mlp_pre": row(g_mlp_pre),
        "g_mlp_post": row(g_mlp_post),
    }


@jax.jit
def kernel(x_prompt, x_sample, state_pool, state_conv, state_ssm, w_in, pool_w, pool_scale, conv_w, conv_b,
           dt_bias, a_log, d_skip, ssd_norm, w_pool_proj, w_ssd_proj, w_out, g_mix_pre, g_mix_post,
           w_up, w_down, g_mlp_pre, g_mlp_post):
    bp, lp, _ = x_prompt.shape
    bs, ls, _ = x_sample.shape
    depth = w_in.shape[0]
    past_len = 1024
    zero_pool = jnp.zeros((bp, POOL_KEEP, D_MODEL), F32)
    zero_conv = jnp.zeros((bp, CONV_W - 1, D_CONV), F32)
    zero_ssm = jnp.zeros((bp, N_HEADS, HEAD_DIM, D_STATE), F32)
    yp = x_prompt.reshape(bp * lp, D_MODEL)
    ys = x_sample.reshape(bs * ls, D_MODEL)
    outs = [[] for _ in range(6)]
    for li in range(depth):
        p = _prep_layer(w_in[li], pool_w[li], pool_scale[li], conv_w[li], conv_b[li], dt_bias[li], a_log[li],
                        d_skip[li], ssd_norm[li], w_pool_proj[li], w_ssd_proj[li], w_out[li], g_mix_pre[li],
                        g_mix_post[li], w_up[li], w_down[li], g_mlp_pre[li], g_mlp_post[li])
        yp, pool_p, conv_p, ssm_p = _layer_group(yp, zero_pool, zero_conv, zero_ssm, 0, p, b=bp, l=lp)
        ys, pool_s, conv_s, ssm_s = _layer_group(ys, state_pool[li], state_conv[li], state_ssm[li], past_len, p,
                                                 b=bs, l=ls)
        for lst, v in zip(outs, (pool_p, conv_p, ssm_p, pool_s, conv_s, ssm_s)):
            lst.append(v)
    return (yp.reshape(bp, lp, D_MODEL), ys.reshape(bs, ls, D_MODEL)) + tuple(jnp.stack(v) for v in outs)
```

```python
import functools

import jax
import jax.numpy as jnp
from jax import lax
from jax.experimental import pallas as pl
from jax.experimental.pallas import tpu as pltpu

F32 = jnp.float32
BF16 = jnp.bfloat16

D_MODEL = 2048
EPS = 1e-6
POOL_WINDOWS = (2, 4, 8, 16)
POOL_GROUP = D_MODEL // len(POOL_WINDOWS)
POOL_KEEP = max(POOL_WINDOWS) - 1
POOL_HALO = 16
D_INNER = 2 * D_MODEL
HEAD_DIM = 64
N_HEADS = D_INNER // HEAD_DIM
N_GROUPS = 8
HEADS_PER_GROUP = N_HEADS // N_GROUPS
GROUP_W = D_INNER // N_GROUPS
D_STATE = 128
CONV_W = 4
D_CONV = D_INNER + 2 * N_GROUPS * D_STATE
CONV_HALO = 8
SSD_Q = 128
D_FF = 4 * D_MODEL
COL_XBC = 0
COL_POOL = COL_XBC + D_CONV
COL_Z = COL_POOL + D_MODEL
COL_G0 = COL_Z + D_INNER
COL_G1 = COL_G0 + D_MODEL
P_COLS = COL_G1 + D_MODEL
DT_LANES = 128
HEAD_LANE_STRIDE = DT_LANES // N_GROUPS
SEL_ROW0 = DT_LANES - HEAD_LANE_STRIDE
SEL_ROWS = SEL_ROW0 + DT_LANES
SEL_COLS = (HEADS_PER_GROUP + HEADS_PER_GROUP // 2) * 128
NEG_BIG = -1e30
LOG2E = 1.4426950408889634
LANE = 128
VMEM_LIMIT = 56 * 1024 * 1024


def _params(n_axes):
    return pltpu.CompilerParams(dimension_semantics=("arbitrary",) * n_axes, vmem_limit_bytes=VMEM_LIMIT)


def _sigmoid(v):
    return 1.0 / (1.0 + jnp.exp2(v * (-LOG2E)))


def _rms_scale(v, g):
    ms = jnp.mean(v * v, axis=-1, keepdims=True)
    return v * lax.rsqrt(ms + EPS) * g


def _dot(a, b):
    return jnp.dot(a, b, preferred_element_type=F32)


def _in_proj_kernel(x_ref, g_ref, w_ref, wdt_ref, dtb_ref, p_ref, dt_ref, u_ref, *, tn):
    j = pl.program_id(1)

    @pl.when(j == 0)
    def _():
        u = _rms_scale(x_ref[...], g_ref[...]).astype(BF16)
        u_ref[...] = u
        raw = _dot(u, wdt_ref[...]) + dtb_ref[...]
        dt_ref[...] = jnp.maximum(raw, 0.0) + jnp.log1p(jnp.exp(-jnp.abs(raw)))

    acc = _dot(u_ref[...], w_ref[...])
    col = j * tn

    @pl.when(col < COL_Z)
    def _():
        p_ref[...] = acc.astype(BF16)

    @pl.when(jnp.logical_and(col >= COL_Z, col < COL_G0))
    def _():
        p_ref[...] = (acc * _sigmoid(acc)).astype(BF16)

    @pl.when(col >= COL_G0)
    def _():
        p_ref[...] = _sigmoid(acc).astype(BF16)


def _in_proj(x, g, w, wdt, dtb, *, tm, tn):
    m = x.shape[0]
    return pl.pallas_call(
        functools.partial(_in_proj_kernel, tn=tn),
        grid=(m // tm, P_COLS // tn),
        in_specs=[
            pl.BlockSpec((tm, D_MODEL), lambda i, j: (i, 0)),
            pl.BlockSpec((1, D_MODEL), lambda i, j: (0, 0)),
            pl.BlockSpec((D_MODEL, tn), lambda i, j: (0, j)),
            pl.BlockSpec((D_MODEL, DT_LANES), lambda i, j: (0, 0)),
            pl.BlockSpec((1, DT_LANES), lambda i, j: (0, 0)),
        ],
        out_specs=[
            pl.BlockSpec((tm, tn), lambda i, j: (i, j)),
            pl.BlockSpec((tm, DT_LANES), lambda i, j: (i, 0)),
        ],
        out_shape=[jax.ShapeDtypeStruct((m, P_COLS), BF16), jax.ShapeDtypeStruct((m, DT_LANES), F32)],
        scratch_shapes=[pltpu.VMEM((tm, D_MODEL), BF16)],
        compiler_params=_params(2),
        name="in_proj",
    )(x, g, w, wdt, dtb)


def _pool_kernel(p_ref, g0_ref, sth_ref, stl_ref, pw_ref, ps_ref, wpp_ref, o_ref, hh_ref, hl_ref, a_ref,
                 *, lt, pos0):
    ti = pl.program_id(1)

    @pl.when(ti == 0)
    def _():
        hh_ref[...] = sth_ref[0]
        hl_ref[...] = stl_ref[0]

    cur = p_ref[...]
    diff = lax.broadcasted_iota(jnp.int32, (lt, lt), 0) - lax.broadcasted_iota(jnp.int32, (lt, lt), 1)
    dist = (lax.broadcasted_iota(jnp.int32, (lt, POOL_HALO), 0) + POOL_HALO
            - lax.broadcasted_iota(jnp.int32, (lt, POOL_HALO), 1))
    pos = pos0 + ti * lt + lax.broadcasted_iota(jnp.int32, (lt, 1), 0)
    for g, w in enumerate(POOL_WINDOWS):
        cs = slice(g * POOL_GROUP, (g + 1) * POOL_GROUP)
        band = jnp.where(diff >= 0, jnp.where(diff < w, 1.0, 0.0), 0.0).astype(BF16)
        hband = jnp.where(dist < w, 1.0, 0.0).astype(BF16)
        xg = cur[:, cs]
        win = _dot(band, xg) + _dot(hband, hh_ref[:, cs]) + _dot(hband, hl_ref[:, cs])
        cnt = jnp.minimum(w, pos + 1).astype(F32)
        d = win * (1.0 / cnt) - xg.astype(F32)
        a_ref[:, cs] = (_dot(d.astype(BF16), pw_ref[g]) * ps_ref[:, cs]).astype(BF16)
    o_ref[...] = (_dot(a_ref[...], wpp_ref[...]) * g0_ref[...].astype(F32)).astype(BF16)
    hh_ref[...] = cur[lt - POOL_HALO:, :]
    hl_ref[...] = jnp.zeros_like(hl_ref)


def _pool_branch(pall, st_hi, st_lo, pool_w, pool_scale, wpp, *, b, l, lt, pos0):
    nt = l // lt
    m = b * l
    return pl.pallas_call(
        functools.partial(_pool_kernel, lt=lt, pos0=pos0),
        grid=(b, nt),
        in_specs=[
            pl.BlockSpec((lt, D_MODEL), lambda bi, ti: (bi * nt + ti, COL_POOL // D_MODEL)),
            pl.BlockSpec((lt, D_MODEL), lambda bi, ti: (bi * nt + ti, COL_G0 // D_MODEL)),
            pl.BlockSpec((1, POOL_HALO, D_MODEL), lambda bi, ti: (bi, 0, 0)),
            pl.BlockSpec((1, POOL_HALO, D_MODEL), lambda bi, ti: (bi, 0, 0)),
            pl.BlockSpec((len(POOL_WINDOWS), POOL_GROUP, POOL_GROUP), lambda bi, ti: (0, 0, 0)),
            pl.BlockSpec((1, D_MODEL), lambda bi, ti: (0, 0)),
            pl.BlockSpec((D_MODEL, D_MODEL), lambda bi, ti: (0, 0)),
        ],
        out_specs=pl.BlockSpec((lt, D_MODEL), lambda bi, ti: (bi * nt + ti, 0)),
        out_shape=jax.ShapeDtypeStruct((m, D_MODEL), BF16),
        scratch_shapes=[
            pltpu.VMEM((POOL_HALO, D_MODEL), BF16),
            pltpu.VMEM((POOL_HALO, D_MODEL), BF16),
            pltpu.VMEM((lt, D_MODEL), BF16),
        ],
        compiler_params=_params(2),
        name="pool_branch",
    )(pall, pall, st_hi, st_lo, pool_w, pool_scale, wpp)


def _split3(v):
    hi = v.astype(BF16)
    r1 = v - hi.astype(F32)
    mid = r1.astype(BF16)
    lo = (r1 - mid.astype(F32)).astype(BF16)
    return hi, mid, lo


def _ssd_kernel(xbc_ref, zs_ref, dt_ref, stc_ref, sts_ref, cw_ref, cb_ref, alog_ref, dsk_ref, nw_ref, sel_ref,
                o_ref, ns_ref, xc_ref, xa_ref, ht_ref, dtt_ref, cumt_ref, colb_ref, yst_ref, y_ref, *, lv, nt):
    q = SSD_Q
    ti = pl.program_id(1)

    @pl.when(ti == 0)
    def _():
        xc_ref[0:CONV_HALO, :] = stc_ref[0]
        for g in range(N_GROUPS):
            st = sts_ref[0, g * HEADS_PER_GROUP:(g + 1) * HEADS_PER_GROUP]
            ht_ref[g] = st.reshape(GROUP_W, D_STATE).T

    if lv < q:
        xa_ref[lv:q, :] = jnp.zeros((q - lv, D_CONV), F32)
    cwid = 256

    def conv_body(c, carry):
        cs = pl.ds(pl.multiple_of(c * cwid, cwid), cwid)
        xc_ref[CONV_HALO:CONV_HALO + lv, cs] = xbc_ref[:, cs].astype(F32)
        full = xc_ref[0:CONV_HALO + lv, cs]
        acc = full[CONV_HALO:] * cw_ref[CONV_W - 1:CONV_W, cs] + cb_ref[:, cs]
        for k in range(1, CONV_W):
            tap = CONV_W - 1 - k
            acc = acc + pltpu.roll(full, k, axis=0)[CONV_HALO:] * cw_ref[tap:tap + 1, cs]
        xa_ref[0:lv, cs] = acc * _sigmoid(acc)
        return carry

    lax.fori_loop(0, D_CONV // cwid, conv_body, 0)
    xc_ref[0:CONV_HALO, :] = xc_ref[lv:lv + CONV_HALO, :]

    if lv < q:
        dt_tm = jnp.concatenate([dt_ref[...], jnp.zeros((q - lv, DT_LANES), F32)], axis=0)
    else:
        dt_tm = dt_ref[...]
    a2 = dt_tm * (-LOG2E * jnp.exp(alog_ref[...]))
    row_i = lax.broadcasted_iota(jnp.int32, (q, q), 0)
    col_i = lax.broadcasted_iota(jnp.int32, (q, q), 1)
    causal = row_i >= col_i
    lower = jnp.where(causal, 1.0, 0.0).astype(BF16)
    hi, mid, lo = _split3(a2)
    cum = _dot(lower, hi) + _dot(lower, mid) + _dot(lower, lo)
    c2 = jnp.concatenate(_split3(cum)[0:2], axis=1)
    cumt_ref[...] = cum.T
    dtt_ref[...] = dt_tm.T
    lane_row = lax.broadcasted_iota(jnp.int32, (1, LANE), 1)
    half0 = jnp.where(lane_row < HEAD_DIM, 1.0, 0.0).astype(BF16)
    half1 = jnp.where(lane_row >= HEAD_DIM, 1.0, 0.0).astype(BF16)

    def group_body(g, carry):
        go = g * GROUP_W
        bg = xa_ref[:, pl.ds(pl.multiple_of(D_INNER + g * D_STATE, D_STATE), D_STATE)]
        cg = xa_ref[:, pl.ds(pl.multiple_of(D_INNER + N_GROUPS * D_STATE + g * D_STATE, D_STATE), D_STATE)]
        cg16 = cg.astype(BF16)
        cb = lax.dot_general(cg16, bg.astype(BF16), (((1,), (1,)), ((), ())), preferred_element_type=F32)
        bt = bg.T
        selg = sel_ref[pl.ds(pl.multiple_of(SEL_ROW0 - HEAD_LANE_STRIDE * g, HEAD_LANE_STRIDE), DT_LANES), :]
        colb_ref[...] = _dot(c2, jnp.concatenate([selg, selg], axis=0))
        yst_ref[...] = _dot(cg16, ht_ref[g].astype(BF16))
        for j in range(HEADS_PER_GROUP // 2):
            pair = slice(j * LANE, (j + 1) * LANE)
            mixes = []
            bws = []
            lasts = []
            for e in range(2):
                r = 2 * j + e
                h = g * HEAD_LANE_STRIDE + r
                cum_row = cumt_ref[pl.ds(h, 1), :]
                dt_row = dtt_ref[pl.ds(h, 1), :]
                cum_col = colb_ref[:, r * LANE:(r + 1) * LANE]
                decay = jnp.exp2(jnp.where(causal, cum_col - cum_row, NEG_BIG))
                mixes.append((cb * decay * dt_row).astype(BF16))
                last = cum_row[:, q - 1:q]
                bws.append((bt * (jnp.exp2(last - cum_row) * dt_row)).astype(BF16))
                lasts.append(jnp.exp2(last))
            xs = xa_ref[:, pl.ds(pl.multiple_of(go + j * LANE, LANE), LANE)]
            xs16 = xs.astype(BF16)
            rhs = jnp.concatenate([xs16 * half0, xs16 * half1], axis=0)
            dsk = dsk_ref[:, pl.ds(pl.multiple_of(go + j * LANE, LANE), LANE)]
            cum_pair = colb_ref[:, (HEADS_PER_GROUP + j) * LANE:(HEADS_PER_GROUP + j + 1) * LANE]
            y_ref[:, pair] = (_dot(jnp.concatenate(mixes, axis=1), rhs)
                              + yst_ref[:, pair] * jnp.exp2(cum_pair) + dsk * xs)
            keep = jnp.where(lane_row < HEAD_DIM, lasts[0], lasts[1])
            ht_ref[g, :, pair] = ht_ref[g, :, pair] * keep + _dot(jnp.concatenate(bws, axis=1), rhs)
        gs = pl.ds(pl.multiple_of(go, GROUP_W), GROUP_W)
        gated = y_ref[0:lv, :] * zs_ref[:, gs].astype(F32)
        o_ref[:, gs] = _rms_scale(gated, nw_ref[:, gs]).astype(BF16)
        return carry

    lax.fori_loop(0, N_GROUPS, group_body, 0)

    @pl.when(ti == nt - 1)
    def _():
        for g in range(N_GROUPS):
            ns_ref[0, g * HEADS_PER_GROUP:(g + 1) * HEADS_PER_GROUP] = (
                ht_ref[g].T.reshape(HEADS_PER_GROUP, HEAD_DIM, D_STATE))


def _ssd_branch(pall, dt, st_conv, st_ssm, conv_w, conv_b, alog_row, dskip_row, norm_w, sel, *, b, l):
    lv = min(l, SSD_Q)
    nt = l // lv
    m = b * l
    return pl.pallas_call(
        functools.partial(_ssd_kernel, lv=lv, nt=nt),
        grid=(b, nt),
        in_specs=[
            pl.BlockSpec((lv, D_CONV), lambda bi, ti: (bi * nt + ti, COL_XBC // D_CONV)),
            pl.BlockSpec((lv, D_INNER), lambda bi, ti: (bi * nt + ti, COL_Z // D_INNER)),
            pl.BlockSpec((lv, DT_LANES), lambda bi, ti: (bi * nt + ti, 0)),
            pl.BlockSpec((1, CONV_HALO, D_CONV), lambda bi, ti: (bi, 0, 0)),
            pl.BlockSpec((1, N_HEADS, HEAD_DIM, D_STATE), lambda bi, ti: (bi, 0, 0, 0)),
            pl.BlockSpec((CONV_W, D_CONV), lambda bi, ti: (0, 0)),
            pl.BlockSpec((1, D_CONV), lambda bi, ti: (0, 0)),
            pl.BlockSpec((1, DT_LANES), lambda bi, ti: (0, 0)),
            pl.BlockSpec((1, D_INNER), lambda bi, ti: (0, 0)),
            pl.BlockSpec((1, D_INNER), lambda bi, ti: (0, 0)),
            pl.BlockSpec((SEL_ROWS, SEL_COLS), lambda bi, ti: (0, 0)),
        ],
        out_specs=[
            pl.BlockSpec((lv, D_INNER), lambda bi, ti: (bi * nt + ti, 0)),
            pl.BlockSpec((1, N_HEADS, HEAD_DIM, D_STATE), lambda bi, ti: (bi, 0, 0, 0)),
        ],
        out_shape=[
            jax.ShapeDtypeStruct((m, D_INNER), BF16),
            jax.ShapeDtypeStruct((b, N_HEADS, HEAD_DIM, D_STATE), F32),
        ],
        scratch_shapes=[
            pltpu.VMEM((CONV_HALO + SSD_Q, D_CONV), F32),
            pltpu.VMEM((SSD_Q, D_CONV), F32),
            pltpu.VMEM((N_GROUPS, D_STATE, GROUP_W), F32),
            pltpu.VMEM((DT_LANES, SSD_Q), F32),
            pltpu.VMEM((DT_LANES, SSD_Q), F32),
            pltpu.VMEM((SSD_Q, SEL_COLS), F32),
            pltpu.VMEM((SSD_Q, GROUP_W), F32),
            pltpu.VMEM((SSD_Q, GROUP_W), F32),
        ],
        compiler_params=_params(2),
        name="ssd_branch",
    )(pall, pall, dt, st_conv, st_ssm, conv_w, conv_b, alog_row, dskip_row, norm_w, sel)


def _mix_out_kernel(a_ref, ws_ref, bp_ref, g1_ref, wo_ref, gp_ref, x_ref, o_ref, acc_ref, *, nk):
    k = pl.program_id(1)

    @pl.when(k == 0)
    def _():
        acc_ref[...] = jnp.zeros_like(acc_ref)

    acc_ref[...] += _dot(a_ref[...], ws_ref[...])

    @pl.when(k == nk - 1)
    def _():
        merged = (bp_ref[...].astype(F32) + g1_ref[...].astype(F32) * acc_ref[...]).astype(BF16)
        out = _dot(merged, wo_ref[...])
        o_ref[...] = x_ref[...] + _rms_scale(out, gp_ref[...])


def _mix_out(a_ssd, w_ssd, bp, pall, w_out, g_post, x, *, tm, tk):
    m = x.shape[0]
    nk = D_INNER // tk
    return pl.pallas_call(
        functools.partial(_mix_out_kernel, nk=nk),
        grid=(m // tm, nk),
        in_specs=[
            pl.BlockSpec((tm, tk), lambda i, k: (i, k)),
            pl.BlockSpec((tk, D_MODEL), lambda i, k: (k, 0)),
            pl.BlockSpec((tm, D_MODEL), lambda i, k: (i, 0)),
            pl.BlockSpec((tm, D_MODEL), lambda i, k: (i, COL_G1 // D_MODEL)),
            pl.BlockSpec((D_MODEL, D_MODEL), lambda i, k: (0, 0)),
            pl.BlockSpec((1, D_MODEL), lambda i, k: (0, 0)),
            pl.BlockSpec((tm, D_MODEL), lambda i, k: (i, 0)),
        ],
        out_specs=pl.BlockSpec((tm, D_MODEL), lambda i, k: (i, 0)),
        out_shape=jax.ShapeDtypeStruct((m, D_MODEL), F32),
        scratch_shapes=[pltpu.VMEM((tm, D_MODEL), F32)],
        compiler_params=_params(2),
        name="mix_out",
    )(a_ssd, w_ssd, bp, pall, w_out, g_post, x)


def _mlp_kernel(x_ref, gpre_ref, wu_ref, wd_ref, gpost_ref, o_ref, u_ref, acc_ref, *, nf):
    j = pl.program_id(1)

    @pl.when(j == 0)
    def _():
        u_ref[...] = _rms_scale(x_ref[...], gpre_ref[...]).astype(BF16)
        acc_ref[...] = jnp.zeros_like(acc_ref)

    hid = jnp.maximum(_dot(u_ref[...], wu_ref[...]), 0.0)
    acc_ref[...] += _dot((hid * hid).astype(BF16), wd_ref[...])

    @pl.when(j == nf - 1)
    def _():
        o_ref[...] = x_ref[...] + _rms_scale(acc_ref[...], gpost_ref[...])


def _mlp(x, g_pre, w_up, w_down, g_post, *, tm, tf):
    m = x.shape[0]
    nf = D_FF // tf
    return pl.pallas_call(
        functools.partial(_mlp_kernel, nf=nf),
        grid=(m // tm, nf),
        in_specs=[
            pl.BlockSpec((tm, D_MODEL), lambda i, j: (i, 0)),
            pl.BlockSpec((1, D_MODEL), lambda i, j: (0, 0)),
            pl.BlockSpec((D_MODEL, tf), lambda i, j: (0, j)),
            pl.BlockSpec((tf, D_MODEL), lambda i, j: (j, 0)),
            pl.BlockSpec((1, D_MODEL), lambda i, j: (0, 0)),
        ],
        out_specs=pl.BlockSpec((tm, D_MODEL), lambda i, j: (i, 0)),
        out_shape=jax.ShapeDtypeStruct((m, D_MODEL), F32),
        scratch_shapes=[pltpu.VMEM((tm, D_MODEL), BF16), pltpu.VMEM((tm, D_MODEL), F32)],
        compiler_params=_params(2),
        name="mlp",
    )(x, g_pre, w_up, w_down, g_post)


def _tiles(m):
    return min(m, 512)


def _layer_group(x, st_pool, st_conv, st_ssm, pos0, p, *, b, l):
    m = b * l
    tm = _tiles(m)
    pall, dt = _in_proj(x, p["g_mix_pre"], p["w_in"], p["w_dt"], p["dt_bias"], tm=tm, tn=1024)

    st16 = jnp.pad(st_pool, ((0, 0), (POOL_HALO - POOL_KEEP, 0), (0, 0)))
    st_hi = st16.astype(BF16)
    st_lo = (st16 - st_hi.astype(F32)).astype(BF16)
    bp = _pool_branch(pall, st_hi, st_lo, p["pool_w"], p["pool_scale"], p["w_pool_proj"],
                      b=b, l=l, lt=min(l, 256), pos0=pos0)

    stc = jnp.pad(st_conv, ((0, 0), (CONV_HALO - (CONV_W - 1), 0), (0, 0)))
    a_ssd, new_ssm = _ssd_branch(pall, dt, stc, st_ssm, p["conv_w"], p["conv_b"], p["a_log"], p["d_skip"],
                                 p["ssd_norm"], _lane_selector(), b=b, l=l)

    x = _mix_out(a_ssd, p["w_ssd_proj"], bp, pall, p["w_out"], p["g_mix_post"], x, tm=min(m, 256), tk=1024)
    x = _mlp(x, p["g_mlp_pre"], p["w_up"], p["w_down"], p["g_mlp_post"], tm=tm, tf=1024)

    p3 = pall.reshape(b, l, P_COLS)
    new_pool = p3[:, l - POOL_KEEP:, COL_POOL:COL_POOL + D_MODEL].astype(F32)
    new_conv = p3[:, l - (CONV_W - 1):, COL_XBC:COL_XBC + D_CONV].astype(F32)
    return x, new_pool, new_conv, new_ssm


def _heads_to_lanes(v):
    lead = v.shape[:-1]
    v = v.reshape(lead + (N_GROUPS, HEADS_PER_GROUP))
    v = jnp.pad(v, [(0, 0)] * len(lead) + [(0, 0), (0, HEAD_LANE_STRIDE - HEADS_PER_GROUP)])
    return v.reshape(lead + (DT_LANES,))


def _lane_selector():
    j = lax.broadcasted_iota(jnp.int32, (SEL_ROWS, SEL_COLS), 0)
    c = lax.broadcasted_iota(jnp.int32, (SEL_ROWS, SEL_COLS), 1)
    blk = c // LANE
    src = jnp.where(blk < HEADS_PER_GROUP, blk, 2 * (blk - HEADS_PER_GROUP) + (c % LANE) // HEAD_DIM)
    return jnp.where(j == SEL_ROW0 + src, 1.0, 0.0).astype(BF16)


def _prep_layer(w_in, pool_w, pool_scale, conv_w, conv_b, dt_bias, a_log, d_skip, ssd_norm, w_pool_proj,
                w_ssd_proj, w_out, g_mix_pre, g_mix_post, w_up, w_down, g_mlp_pre, g_mlp_post):
    o_z = D_MODEL
    o_xbc = o_z + D_INNER
    o_dt = o_xbc + D_CONV
    o_gate = o_dt + N_HEADS
    w_main = jnp.concatenate([w_in[:, o_xbc:o_dt], w_in[:, 0:o_z], w_in[:, o_z:o_xbc], w_in[:, o_gate:]],
                             axis=1).astype(BF16)
    w_dt = _heads_to_lanes(w_in[:, o_dt:o_gate]).astype(BF16)
    row = lambda v: v.reshape(1, -1)
    return {
        "w_in": w_main,
        "w_dt": w_dt,
        "dt_bias": _heads_to_lanes(row(dt_bias)),
        "pool_w": pool_w.astype(BF16),
        "pool_scale": row(pool_scale),
        "conv_w": conv_w,
        "conv_b": row(conv_b),
        "a_log": _heads_to_lanes(row(a_log)),
        "d_skip": row(jnp.repeat(d_skip, HEAD_DIM)),
        "ssd_norm": row(ssd_norm),
        "w_pool_proj": w_pool_proj.astype(BF16),
        "w_ssd_proj": w_ssd_proj.astype(BF16),
        "w_out": w_out.astype(BF16),
        "g_mix_pre": row(g_mix_pre),
        "g_mix_post": row(g_mix_post),
        "w_up": w_up.astype(BF16),
        "w_down": w_down.astype(BF16),
        "g_mlp_pre": row(g_mlp_pre),
        "g_mlp_post": row(g_mlp_post),
    }


@jax.jit
def kernel(x_prompt, x_sample, state_pool, state_conv, state_ssm, w_in, pool_w, pool_scale, conv_w, conv_b,
           dt_bias, a_log, d_skip, ssd_norm, w_pool_proj, w_ssd_proj, w_out, g_mix_pre, g_mix_post,
           w_up, w_down, g_mlp_pre, g_mlp_post):
    bp, lp, _ = x_prompt.shape
    bs, ls, _ = x_sample.shape
    depth = w_in.shape[0]
    past_len = 1024
    zero_pool = jnp.zeros((bp, POOL_KEEP, D_MODEL), F32)
    zero_conv = jnp.zeros((bp, CONV_W - 1, D_CONV), F32)
    zero_ssm = jnp.zeros((bp, N_HEADS, HEAD_DIM, D_STATE), F32)
    yp = x_prompt.reshape(bp * lp, D_MODEL)
    ys = x_sample.reshape(bs * ls, D_MODEL)
    outs = [[] for _ in range(6)]
    for li in range(depth):
        p = _prep_layer(w_in[li], pool_w[li], pool_scale[li], conv_w[li], conv_b[li], dt_bias[li], a_log[li],
                        d_skip[li], ssd_norm[li], w_pool_proj[li], w_ssd_proj[li], w_out[li], g_mix_pre[li],
                        g_mix_post[li], w_up[li], w_down[li], g_mlp_pre[li], g_mlp_post[li])
        yp, pool_p, conv_p, ssm_p = _layer_group(yp, zero_pool, zero_conv, zero_ssm, 0, p, b=bp, l=lp)
        ys, pool_s, conv_s, ssm_s = _layer_group(ys, state_pool[li], state_conv[li], state_ssm[li], past_len, p,
                                                 b=bs, l=ls)
        for lst, v in zip(outs, (pool_p, conv_p, ssm_p, pool_s, conv_s, ssm_s)):
            lst.append(v)
    return (yp.reshape(bp, lp, D_MODEL), ys.reshape(bs, ls, D_MODEL)) + tuple(jnp.stack(v) for v in outs)
```

```python
import functools

import jax
import jax.numpy as jnp
from jax import lax
from jax.experimental import pallas as pl
from jax.experimental.pallas import tpu as pltpu

F32 = jnp.float32
BF16 = jnp.bfloat16

D_MODEL = 2048
EPS = 1e-6
POOL_WINDOWS = (2, 4, 8, 16)
POOL_GROUP = D_MODEL // len(POOL_WINDOWS)
POOL_KEEP = max(POOL_WINDOWS) - 1
POOL_HALO = 16
D_INNER = 2 * D_MODEL
HEAD_DIM = 64
N_HEADS = D_INNER // HEAD_DIM
N_GROUPS = 8
HEADS_PER_GROUP = N_HEADS // N_GROUPS
GROUP_W = D_INNER // N_GROUPS
D_STATE = 128
CONV_W = 4
D_CONV = D_INNER + 2 * N_GROUPS * D_STATE
SSD_Q = 128
GROUP_SLOTS = 4
CONV_HALO = 8
D_FF = 4 * D_MODEL
COL_XBC = 0
COL_POOL = COL_XBC + D_CONV
COL_Z = COL_POOL + D_MODEL
COL_G0 = COL_Z + D_INNER
COL_G1 = COL_G0 + D_MODEL
P_COLS = COL_G1 + D_MODEL
SRC_COL_POOL = 0
SRC_COL_XBC = SRC_COL_POOL + D_MODEL + D_INNER
SRC_COL_GATE = SRC_COL_XBC + D_CONV
DT_LANES = 128
HEAD_LANE_STRIDE = DT_LANES // N_GROUPS
SEL_ROW0 = DT_LANES - HEAD_LANE_STRIDE
SEL_ROWS = SEL_ROW0 + DT_LANES
SEL_COLS = (HEADS_PER_GROUP + HEADS_PER_GROUP // 2) * 128
NEG_BIG = -1e30
LOG2E = 1.4426950408889634
LANE = 128
VMEM_LIMIT = 56 * 1024 * 1024


def _params(n_axes):
    return pltpu.CompilerParams(dimension_semantics=("arbitrary",) * n_axes, vmem_limit_bytes=VMEM_LIMIT)


def _sigmoid(v):
    return 1.0 / (1.0 + jnp.exp2(v * (-LOG2E)))


def _rms_scale(v, g):
    ms = jnp.mean(v * v, axis=-1, keepdims=True)
    return v * lax.rsqrt(ms + EPS) * g


def _dot(a, b):
    return jnp.dot(a, b, preferred_element_type=F32)


def _in_proj_kernel(x_ref, g_ref, w_ref, wdt_ref, dtb_ref, p_ref, dt_ref, u_ref, *, tn):
    j = pl.program_id(1)

    @pl.when(j == 0)
    def _():
        u = _rms_scale(x_ref[...], g_ref[...]).astype(BF16)
        u_ref[...] = u
        raw = _dot(u, wdt_ref[...]) + dtb_ref[...]
        dt_ref[...] = jnp.maximum(raw, 0.0) + jnp.log1p(jnp.exp(-jnp.abs(raw)))

    acc = _dot(u_ref[...], w_ref[...])
    col = j * tn
    sig = _sigmoid(acc)
    act = jnp.where(col >= COL_G0, sig, acc * sig)
    p_ref[...] = jnp.where(col < COL_Z, acc, act).astype(BF16)


def _in_proj(x, g, w, wdt, dtb, *, tm, tn):
    m = x.shape[0]
    n_rot = (SRC_COL_GATE - SRC_COL_POOL) // tn
    shift = (SRC_COL_XBC - SRC_COL_POOL) // tn

    def w_block(i, j):
        return (0, jnp.where(j < n_rot, (j + shift) % n_rot, j))

    return pl.pallas_call(
        functools.partial(_in_proj_kernel, tn=tn),
        grid=(m // tm, P_COLS // tn),
        in_specs=[
            pl.BlockSpec((tm, D_MODEL), lambda i, j: (i, 0)),
            pl.BlockSpec((1, D_MODEL), lambda i, j: (0, 0)),
            pl.BlockSpec((D_MODEL, tn), w_block),
            pl.BlockSpec((D_MODEL, DT_LANES), lambda i, j: (0, 0)),
            pl.BlockSpec((1, DT_LANES), lambda i, j: (0, 0)),
        ],
        out_specs=[
            pl.BlockSpec((tm, tn), lambda i, j: (i, j)),
            pl.BlockSpec((tm, DT_LANES), lambda i, j: (i, 0)),
        ],
        out_shape=[jax.ShapeDtypeStruct((m, P_COLS), BF16), jax.ShapeDtypeStruct((m, DT_LANES), F32)],
        scratch_shapes=[pltpu.VMEM((tm, D_MODEL), BF16)],
        compiler_params=_params(2),
        name="in_proj",
    )(x, g, w, wdt, dtb)


def _pool_kernel(p_ref, g0_ref, sth_ref, stl_ref, pw_ref, ps_ref, wpp_ref, o_ref, hh_ref, hl_ref, a_ref,
                 *, lt, pos0):
    ti = pl.program_id(1)

    @pl.when(ti == 0)
    def _():
        hh_ref[...] = sth_ref[0]
        hl_ref[...] = stl_ref[0]

    cur = p_ref[...]
    diff = lax.broadcasted_iota(jnp.int32, (lt, lt), 0) - lax.broadcasted_iota(jnp.int32, (lt, lt), 1)
    dist = (lax.broadcasted_iota(jnp.int32, (lt, POOL_HALO), 0) + POOL_HALO
            - lax.broadcasted_iota(jnp.int32, (lt, POOL_HALO), 1))
    pos = pos0 + ti * lt + lax.broadcasted_iota(jnp.int32, (lt, 1), 0)
    for g, w in enumerate(POOL_WINDOWS):
        cs = slice(g * POOL_GROUP, (g + 1) * POOL_GROUP)
        band = jnp.where(diff >= 0, jnp.where(diff < w, 1.0, 0.0), 0.0).astype(BF16)
        hband = jnp.where(dist < w, 1.0, 0.0).astype(BF16)
        xg = cur[:, cs]
        win = _dot(band, xg) + _dot(hband, hh_ref[:, cs]) + _dot(hband, hl_ref[:, cs])
        cnt = jnp.minimum(w, pos + 1).astype(F32)
        d = win * (1.0 / cnt) - xg.astype(F32)
        a_ref[:, cs] = (_dot(d.astype(BF16), pw_ref[g]) * ps_ref[:, cs]).astype(BF16)
    o_ref[...] = (_dot(a_ref[...], wpp_ref[...]) * g0_ref[...].astype(F32)).astype(BF16)
    hh_ref[...] = cur[lt - POOL_HALO:, :]
    hl_ref[...] = jnp.zeros_like(hl_ref)


def _pool_branch(pall, st_hi, st_lo, pool_w, pool_scale, wpp, *, b, l, lt, pos0):
    nt = l // lt
    m = b * l
    return pl.pallas_call(
        functools.partial(_pool_kernel, lt=lt, pos0=pos0),
        grid=(b, nt),
        in_specs=[
            pl.BlockSpec((lt, D_MODEL), lambda bi, ti: (bi * nt + ti, COL_POOL // D_MODEL)),
            pl.BlockSpec((lt, D_MODEL), lambda bi, ti: (bi * nt + ti, COL_G0 // D_MODEL)),
            pl.BlockSpec((1, POOL_HALO, D_MODEL), lambda bi, ti: (bi, 0, 0)),
            pl.BlockSpec((1, POOL_HALO, D_MODEL), lambda bi, ti: (bi, 0, 0)),
            pl.BlockSpec((len(POOL_WINDOWS), POOL_GROUP, POOL_GROUP), lambda bi, ti: (0, 0, 0)),
            pl.BlockSpec((1, D_MODEL), lambda bi, ti: (0, 0)),
            pl.BlockSpec((D_MODEL, D_MODEL), lambda bi, ti: (0, 0)),
        ],
        out_specs=pl.BlockSpec((lt, D_MODEL), lambda bi, ti: (bi * nt + ti, 0)),
        out_shape=jax.ShapeDtypeStruct((m, D_MODEL), BF16),
        scratch_shapes=[
            pltpu.VMEM((POOL_HALO, D_MODEL), BF16),
            pltpu.VMEM((POOL_HALO, D_MODEL), BF16),
            pltpu.VMEM((lt, D_MODEL), BF16),
        ],
        compiler_params=_params(2),
        name="pool_branch",
    )(pall, pall, st_hi, st_lo, pool_w, pool_scale, wpp)


def _split3(v):
    hi = v.astype(BF16)
    r1 = v - hi.astype(F32)
    mid = r1.astype(BF16)
    lo = (r1 - mid.astype(F32)).astype(BF16)
    return hi, mid, lo


def _ssd_kernel(xbc_ref, zs_ref, dt_ref, stc_ref, sts_ref, cw_ref, cb_ref, alog_ref, dsk_ref, nw_ref, sel_ref,
                o_ref, ns_ref, xc_ref, xa_ref, ht_ref, dtt_ref, cumt_ref, colb2_ref, yst2_ref, y2_ref, *, lv, nt):
    q = SSD_Q
    ti = pl.program_id(1)

    @pl.when(ti == 0)
    def _():
        xc_ref[0:CONV_HALO, :] = stc_ref[0]
        for g in range(N_GROUPS):
            sg = sts_ref[0, g * HEADS_PER_GROUP:(g + 1) * HEADS_PER_GROUP]
            ht_ref[g] = sg.reshape(GROUP_W, D_STATE).T

    if lv < q:
        xa_ref[lv:q, :] = jnp.zeros((q - lv, D_CONV), F32)
    cwid = 256

    def conv_body(c, carry):
        cs = pl.ds(pl.multiple_of(c * cwid, cwid), cwid)
        xc_ref[CONV_HALO:CONV_HALO + lv, cs] = xbc_ref[:, cs].astype(F32)
        full = xc_ref[0:CONV_HALO + lv, cs]
        acc = full[CONV_HALO:] * cw_ref[CONV_W - 1:CONV_W, cs] + cb_ref[:, cs]
        for k in range(1, CONV_W):
            tap = CONV_W - 1 - k
            acc = acc + pltpu.roll(full, k, axis=0)[CONV_HALO:] * cw_ref[tap:tap + 1, cs]
        xa_ref[0:lv, cs] = acc * _sigmoid(acc)
        return carry

    lax.fori_loop(0, D_CONV // cwid, conv_body, 0)
    xc_ref[0:CONV_HALO, :] = xc_ref[lv:lv + CONV_HALO, :]

    if lv < q:
        dt_tm = jnp.concatenate([dt_ref[...], jnp.zeros((q - lv, DT_LANES), F32)], axis=0)
    else:
        dt_tm = dt_ref[...]
    a2 = dt_tm * (-LOG2E * jnp.exp(alog_ref[...]))
    row_i = lax.broadcasted_iota(jnp.int32, (q, q), 0)
    col_i = lax.broadcasted_iota(jnp.int32, (q, q), 1)
    causal = row_i >= col_i
    lower = jnp.where(causal, 1.0, 0.0).astype(BF16)
    hi, mid, lo = _split3(a2)
    cum = _dot(lower, hi) + _dot(lower, mid) + _dot(lower, lo)
    c2 = jnp.concatenate(_split3(cum)[0:2], axis=1)
    cumt_ref[...] = cum.T
    dtt_ref[...] = dt_tm.T
    lane_row = lax.broadcasted_iota(jnp.int32, (1, LANE), 1)
    half0 = jnp.where(lane_row < HEAD_DIM, 1.0, 0.0).astype(BF16)
    half1 = jnp.where(lane_row >= HEAD_DIM, 1.0, 0.0).astype(BF16)

    def one_group(g, slot):
        colb_ref, yst_ref, y_ref = colb2_ref.at[slot], yst2_ref.at[slot], y2_ref.at[slot]
        go = g * GROUP_W
        bg = xa_ref[:, pl.ds(pl.multiple_of(D_INNER + g * D_STATE, D_STATE), D_STATE)]
        cg = xa_ref[:, pl.ds(pl.multiple_of(D_INNER + N_GROUPS * D_STATE + g * D_STATE, D_STATE), D_STATE)]
        cg16 = cg.astype(BF16)
        cb = lax.dot_general(cg16, bg.astype(BF16), (((1,), (1,)), ((), ())), preferred_element_type=F32)
        bt = bg.T
        selg = sel_ref[pl.ds(pl.multiple_of(SEL_ROW0 - HEAD_LANE_STRIDE * g, HEAD_LANE_STRIDE), DT_LANES), :]
        colb_ref[...] = _dot(c2, jnp.concatenate([selg, selg], axis=0))
        yst_ref[...] = _dot(cg16, ht_ref[g].astype(BF16))
        for j in range(HEADS_PER_GROUP // 2):
            pair = slice(j * LANE, (j + 1) * LANE)
            mixes = []
            bws = []
            lasts = []
            for e in range(2):
                r = 2 * j + e
                h = g * HEAD_LANE_STRIDE + r
                cum_row = cumt_ref[pl.ds(h, 1), :]
                last = cum_row[:, q - 1:q]
                src_row = cum_row - jnp.log2(dtt_ref[pl.ds(h, 1), :])
                cum_col = colb_ref[:, r * LANE:(r + 1) * LANE]
                mixes.append((cb * jnp.exp2(jnp.where(causal, cum_col - src_row, NEG_BIG))).astype(BF16))
                bws.append((bt * jnp.exp2(last - src_row)).astype(BF16))
                lasts.append(jnp.exp2(last))
            xs = xa_ref[:, pl.ds(pl.multiple_of(go + j * LANE, LANE), LANE)]
            xs16 = xs.astype(BF16)
            rhs = jnp.concatenate([xs16 * half0, xs16 * half1], axis=0)
            dsk = dsk_ref[:, pl.ds(pl.multiple_of(go + j * LANE, LANE), LANE)]
            cum_pair = colb_ref[:, (HEADS_PER_GROUP + j) * LANE:(HEADS_PER_GROUP + j + 1) * LANE]
            y_ref[:, pair] = (_dot(jnp.concatenate(mixes, axis=1), rhs)
                              + yst_ref[:, pair] * jnp.exp2(cum_pair) + dsk * xs)
            keep = jnp.where(lane_row < HEAD_DIM, lasts[0], lasts[1])
            ht_ref[g, :, pair] = ht_ref[g, :, pair] * keep + _dot(jnp.concatenate(bws, axis=1), rhs)
        gs = pl.ds(pl.multiple_of(go, GROUP_W), GROUP_W)
        gated = y_ref[0:lv, :] * zs_ref[:, gs].astype(F32)
        o_ref[:, gs] = _rms_scale(gated, nw_ref[:, gs]).astype(BF16)

    def group_batch_body(i, carry):
        for slot in range(GROUP_SLOTS):
            one_group(GROUP_SLOTS * i + slot, slot)
        return carry

    lax.fori_loop(0, N_GROUPS // GROUP_SLOTS, group_batch_body, 0)

    @pl.when(ti == nt - 1)
    def _():
        for g in range(N_GROUPS):
            ns_ref[0, g * HEADS_PER_GROUP:(g + 1) * HEADS_PER_GROUP] = (
                ht_ref[g].T.reshape(HEADS_PER_GROUP, HEAD_DIM, D_STATE))


def _ssd_branch(pall, dt, st_conv, st_ssm, conv_w, conv_b, alog_row, dskip_row, norm_w, sel, *, b, l):
    lv = min(l, SSD_Q)
    nt = l // lv
    m = b * l
    return pl.pallas_call(
        functools.partial(_ssd_kernel, lv=lv, nt=nt),
        grid=(b, nt),
        in_specs=[
            pl.BlockSpec((lv, D_CONV), lambda bi, ti: (bi * nt + ti, COL_XBC // D_CONV)),
            pl.BlockSpec((lv, D_INNER), lambda bi, ti: (bi * nt + ti, COL_Z // D_INNER)),
            pl.BlockSpec((lv, DT_LANES), lambda bi, ti: (bi * nt + ti, 0)),
            pl.BlockSpec((1, CONV_HALO, D_CONV), lambda bi, ti: (bi, 0, 0)),
            pl.BlockSpec((1, N_HEADS, HEAD_DIM, D_STATE), lambda bi, ti: (bi, 0, 0, 0)),
            pl.BlockSpec((CONV_W, D_CONV), lambda bi, ti: (0, 0)),
            pl.BlockSpec((1, D_CONV), lambda bi, ti: (0, 0)),
            pl.BlockSpec((1, DT_LANES), lambda bi, ti: (0, 0)),
            pl.BlockSpec((1, D_INNER), lambda bi, ti: (0, 0)),
            pl.BlockSpec((1, D_INNER), lambda bi, ti: (0, 0)),
            pl.BlockSpec((SEL_ROWS, SEL_COLS), lambda bi, ti: (0, 0)),
        ],
        out_specs=[
            pl.BlockSpec((lv, D_INNER), lambda bi, ti: (bi * nt + ti, 0)),
            pl.BlockSpec((1, N_HEADS, HEAD_DIM, D_STATE), lambda bi, ti: (bi, 0, 0, 0)),
        ],
        out_shape=[
            jax.ShapeDtypeStruct((m, D_INNER), BF16),
            jax.ShapeDtypeStruct((b, N_HEADS, HEAD_DIM, D_STATE), F32),
        ],
        scratch_shapes=[
            pltpu.VMEM((CONV_HALO + SSD_Q, D_CONV), F32),
            pltpu.VMEM((SSD_Q, D_CONV), F32),
            pltpu.VMEM((N_GROUPS, D_STATE, GROUP_W), F32),
            pltpu.VMEM((DT_LANES, SSD_Q), F32),
            pltpu.VMEM((DT_LANES, SSD_Q), F32),
            pltpu.VMEM((GROUP_SLOTS, SSD_Q, SEL_COLS), F32),
            pltpu.VMEM((GROUP_SLOTS, SSD_Q, GROUP_W), F32),
            pltpu.VMEM((GROUP_SLOTS, SSD_Q, GROUP_W), F32),
        ],
        compiler_params=_params(2),
        name="ssd_branch",
    )(pall, pall, dt, st_conv, st_ssm, conv_w, conv_b, alog_row, dskip_row, norm_w, sel)


def _mix_out_kernel(a_ref, ws_ref, bp_ref, g1_ref, wo_ref, gp_ref, x_ref, o_ref):
    br_ssd = _dot(a_ref[...], ws_ref[...])
    merged = (bp_ref[...].astype(F32) + g1_ref[...].astype(F32) * br_ssd).astype(BF16)
    out = _dot(merged, wo_ref[...])
    o_ref[...] = x_ref[...] + _rms_scale(out, gp_ref[...])


def _mix_out(a_ssd, w_ssd, bp, pall, w_out, g_post, x, *, tm):
    m = x.shape[0]
    resident = pl.Buffered(1)
    return pl.pallas_call(
        _mix_out_kernel,
        grid=(m // tm,),
        in_specs=[
            pl.BlockSpec((tm, D_INNER), lambda i: (i, 0)),
            pl.BlockSpec((D_INNER, D_MODEL), lambda i: (0, 0), pipeline_mode=resident),
            pl.BlockSpec((tm, D_MODEL), lambda i: (i, 0)),
            pl.BlockSpec((tm, D_MODEL), lambda i: (i, COL_G1 // D_MODEL)),
            pl.BlockSpec((D_MODEL, D_MODEL), lambda i: (0, 0), pipeline_mode=resident),
            pl.BlockSpec((1, D_MODEL), lambda i: (0, 0)),
            pl.BlockSpec((tm, D_MODEL), lambda i: (i, 0)),
        ],
        out_specs=pl.BlockSpec((tm, D_MODEL), lambda i: (i, 0)),
        out_shape=jax.ShapeDtypeStruct((m, D_MODEL), F32),
        compiler_params=_params(1),
        name="mix_out",
    )(a_ssd, w_ssd, bp, pall, w_out, g_post, x)


def _mlp_kernel(x_ref, gpre_ref, wu_ref, wd_ref, gpost_ref, o_ref, u_ref, acc_ref, *, nf):
    j = pl.program_id(1)

    @pl.when(j == 0)
    def _():
        u_ref[...] = _rms_scale(x_ref[...], gpre_ref[...]).astype(BF16)
        acc_ref[...] = jnp.zeros_like(acc_ref)

    hid = jnp.maximum(_dot(u_ref[...], wu_ref[...]), 0.0)
    acc_ref[...] += _dot((hid * hid).astype(BF16), wd_ref[...])

    @pl.when(j == nf - 1)
    def _():
        o_ref[...] = x_ref[...] + _rms_scale(acc_ref[...], gpost_ref[...])


def _mlp(x, g_pre, w_up, w_down, g_post, *, tm, tf):
    m = x.shape[0]
    nf = D_FF // tf
    return pl.pallas_call(
        functools.partial(_mlp_kernel, nf=nf),
        grid=(m // tm, nf),
        in_specs=[
            pl.BlockSpec((tm, D_MODEL), lambda i, j: (i, 0)),
            pl.BlockSpec((1, D_MODEL), lambda i, j: (0, 0)),
            pl.BlockSpec((D_MODEL, tf), lambda i, j: (0, j)),
            pl.BlockSpec((tf, D_MODEL), lambda i, j: (j, 0)),
            pl.BlockSpec((1, D_MODEL), lambda i, j: (0, 0)),
        ],
        out_specs=pl.BlockSpec((tm, D_MODEL), lambda i, j: (i, 0)),
        out_shape=jax.ShapeDtypeStruct((m, D_MODEL), F32),
        scratch_shapes=[pltpu.VMEM((tm, D_MODEL), BF16), pltpu.VMEM((tm, D_MODEL), F32)],
        compiler_params=_params(2),
        name="mlp",
    )(x, g_pre, w_up, w_down, g_post)


def _tiles(m):
    return min(m, 512)


def _layer_group(x, st_pool, st_conv, st_ssm, pos0, p, *, b, l):
    m = b * l
    tm = _tiles(m)
    pall, dt = _in_proj(x, p["g_mix_pre"], p["w_in"], p["w_dt"], p["dt_bias"], tm=tm, tn=2048)

    st16 = jnp.pad(st_pool, ((0, 0), (POOL_HALO - POOL_KEEP, 0), (0, 0)))
    st_hi = st16.astype(BF16)
    st_lo = (st16 - st_hi.astype(F32)).astype(BF16)
    bp = _pool_branch(pall, st_hi, st_lo, p["pool_w"], p["pool_scale"], p["w_pool_proj"],
                      b=b, l=l, lt=min(l, 256), pos0=pos0)

    stc = jnp.pad(st_conv, ((0, 0), (CONV_HALO - (CONV_W - 1), 0), (0, 0)))
    a_ssd, new_ssm = _ssd_branch(pall, dt, stc, st_ssm, p["conv_w"], p["conv_b"], p["a_log"], p["d_skip"],
                                 p["ssd_norm"], _lane_selector(), b=b, l=l)

    x = _mix_out(a_ssd, p["w_ssd_proj"], bp, pall, p["w_out"], p["g_mix_post"], x, tm=min(m, 256))
    x = _mlp(x, p["g_mlp_pre"], p["w_up"], p["w_down"], p["g_mlp_post"], tm=tm, tf=1024)

    p3 = pall.reshape(b, l, P_COLS)
    new_pool = p3[:, l - POOL_KEEP:, COL_POOL:COL_POOL + D_MODEL].astype(F32)
    new_conv = p3[:, l - (CONV_W - 1):, COL_XBC:COL_XBC + D_CONV].astype(F32)
    return x, new_pool, new_conv, new_ssm


def _heads_to_lanes(v):
    lead = v.shape[:-1]
    v = v.reshape(lead + (N_GROUPS, HEADS_PER_GROUP))
    v = jnp.pad(v, [(0, 0)] * len(lead) + [(0, 0), (0, HEAD_LANE_STRIDE - HEADS_PER_GROUP)])
    return v.reshape(lead + (DT_LANES,))


def _lane_selector():
    j = lax.broadcasted_iota(jnp.int32, (SEL_ROWS, SEL_COLS), 0)
    c = lax.broadcasted_iota(jnp.int32, (SEL_ROWS, SEL_COLS), 1)
    blk = c // LANE
    src = jnp.where(blk < HEADS_PER_GROUP, blk, 2 * (blk - HEADS_PER_GROUP) + (c % LANE) // HEAD_DIM)
    return jnp.where(j == SEL_ROW0 + src, 1.0, 0.0).astype(BF16)


def _prep_layer(w_in, pool_w, pool_scale, conv_w, conv_b, dt_bias, a_log, d_skip, ssd_norm, w_pool_proj,
                w_ssd_proj, w_out, g_mix_pre, g_mix_post, w_up, w_down, g_mlp_pre, g_mlp_post):
    o_z = D_MODEL
    o_xbc = o_z + D_INNER
    o_dt = o_xbc + D_CONV
    o_gate = o_dt + N_HEADS
    w_main = jnp.concatenate([w_in[:, 0:o_dt], w_in[:, o_gate:]], axis=1).astype(BF16)
    w_dt = _heads_to_lanes(w_in[:, o_dt:o_gate]).astype(BF16)
    row = lambda v: v.reshape(1, -1)
    return {
        "w_in": w_main,
        "w_dt": w_dt,
        "dt_bias": _heads_to_lanes(row(dt_bias)),
        "pool_w": pool_w.astype(BF16),
        "pool_scale": row(pool_scale),
        "conv_w": conv_w,
        "conv_b": row(conv_b),
        "a_log": _heads_to_lanes(row(a_log)),
        "d_skip": row(jnp.repeat(d_skip, HEAD_DIM)),
        "ssd_norm": row(ssd_norm),
        "w_pool_proj": w_pool_proj.astype(BF16),
        "w_ssd_proj": w_ssd_proj.astype(BF16),
        "w_out": w_out.astype(BF16),
        "g_mix_pre": row(g_mix_pre),
        "g_mix_post": row(g_mix_post),
        "w_up": w_up.astype(BF16),
        "w_down": w_down.astype(BF16),
        "g_mlp_pre": row(g_mlp_pre),
        "g_mlp_post": row(g_mlp_post),
    }


@jax.jit
def kernel(x_prompt, x_sample, state_pool, state_conv, state_ssm, w_in, pool_w, pool_scale, conv_w, conv_b,
           dt_bias, a_log, d_skip, ssd_norm, w_pool_proj, w_ssd_proj, w_out, g_mix_pre, g_mix_post,
           w_up, w_down, g_mlp_pre, g_mlp_post):
    bp, lp, _ = x_prompt.shape
    bs, ls, _ = x_sample.shape
    depth = w_in.shape[0]
    past_len = 1024
    zero_pool = jnp.zeros((bp, POOL_KEEP, D_MODEL), F32)
    zero_conv = jnp.zeros((bp, CONV_W - 1, D_CONV), F32)
    zero_ssm = jnp.zeros((bp, N_HEADS, HEAD_DIM, D_STATE), F32)
    yp = x_prompt.reshape(bp * lp, D_MODEL)
    ys = x_sample.reshape(bs * ls, D_MODEL)
    outs = [[] for _ in range(6)]
    for li in range(depth):
        p = _prep_layer(w_in[li], pool_w[li], pool_scale[li], conv_w[li], conv_b[li], dt_bias[li], a_log[li],
                        d_skip[li], ssd_norm[li], w_pool_proj[li], w_ssd_proj[li], w_out[li], g_mix_pre[li],
                        g_mix_post[li], w_up[li], w_down[li], g_mlp_pre[li], g_mlp_post[li])
        yp, pool_p, conv_p, ssm_p = _layer_group(yp, zero_pool, zero_conv, zero_ssm, 0, p, b=bp, l=lp)
        ys, pool_s, conv_s, ssm_s = _layer_group(ys, state_pool[li], state_conv[li], state_ssm[li], past_len, p,
                                                 b=bs, l=ls)
        for lst, v in zip(outs, (pool_p, conv_p, ssm_p, pool_s, conv_s, ssm_s)):
            lst.append(v)
    return (yp.reshape(bp, lp, D_MODEL), ys.reshape(bs, ls, D_MODEL)) + tuple(jnp.stack(v) for v in outs)
```

```python
import functools

import jax
import jax.numpy as jnp
from jax import lax
from jax.experimental import pallas as pl
from jax.experimental.pallas import tpu as pltpu

F32 = jnp.float32
BF16 = jnp.bfloat16

D_MODEL = 2048
EPS = 1e-6
POOL_WINDOWS = (2, 4, 8, 16)
POOL_GROUP = D_MODEL // len(POOL_WINDOWS)
POOL_KEEP = max(POOL_WINDOWS) - 1
POOL_HALO = 16
D_INNER = 2 * D_MODEL
HEAD_DIM = 64
N_HEADS = D_INNER // HEAD_DIM
N_GROUPS = 8
HEADS_PER_GROUP = N_HEADS // N_GROUPS
GROUP_W = D_INNER // N_GROUPS
D_STATE = 128
CONV_W = 4
D_CONV = D_INNER + 2 * N_GROUPS * D_STATE
SSD_Q = 128
GROUP_SLOTS = 4
CONV_HALO = 8
D_FF = 4 * D_MODEL
COL_XBC = 0
COL_POOL = COL_XBC + D_CONV
COL_Z = COL_POOL + D_MODEL
COL_G0 = COL_Z + D_INNER
COL_G1 = COL_G0 + D_MODEL
P_COLS = COL_G1 + D_MODEL
SRC_COL_POOL = 0
SRC_COL_XBC = SRC_COL_POOL + D_MODEL + D_INNER
SRC_COL_DT = SRC_COL_XBC + D_CONV
SRC_COL_GATE = SRC_COL_DT + N_HEADS
IN_COLS = SRC_COL_GATE + 2 * D_MODEL
DT_LANES = 128
HEAD_LANE_STRIDE = DT_LANES // N_GROUPS
SEL_ROW0 = DT_LANES - HEAD_LANE_STRIDE
SEL_ROWS = SEL_ROW0 + DT_LANES
SEL_COLS = (HEADS_PER_GROUP + HEADS_PER_GROUP // 2) * 128
NEG_BIG = -1e30
LOG2E = 1.4426950408889634
LANE = 128
VMEM_LIMIT = 56 * 1024 * 1024


def _params(n_axes):
    return pltpu.CompilerParams(dimension_semantics=("arbitrary",) * n_axes, vmem_limit_bytes=VMEM_LIMIT)


def _sigmoid(v):
    return 1.0 / (1.0 + jnp.exp2(v * (-LOG2E)))


def _rms_scale(v, g):
    ms = jnp.mean(v * v, axis=-1, keepdims=True)
    return v * lax.rsqrt(ms + EPS) * g


def _dot(a, b):
    return jnp.dot(a, b, preferred_element_type=F32)


def _w_prep_kernel(a_ref, b_ref, o_ref, *, n_plain, shift):
    j = pl.program_id(2)

    @pl.when(j < n_plain)
    def _():
        o_ref[...] = a_ref[...].astype(BF16)

    @pl.when(j >= n_plain)
    def _():
        wide = jnp.concatenate([a_ref[...], b_ref[...]], axis=1)
        o_ref[...] = pltpu.roll(wide, wide.shape[1] - shift, axis=1)[:, :o_ref.shape[1]].astype(BF16)


def _prep_w_in(w_in, *, tr=512, tn=2048):
    depth = w_in.shape[0]
    n_x = D_CONV // tn
    n_plain = n_x + (D_MODEL + D_INNER) // tn

    def a_block(d, i, j):
        plain = jnp.where(j < n_x, SRC_COL_XBC // tn + j, j - n_x)
        return (d, i, jnp.where(j < n_plain, plain, SRC_COL_DT // tn + (j - n_plain)))

    def b_block(d, i, j):
        return (d, i, (SRC_COL_DT + (jnp.maximum(j - n_plain, 0) + 1) * tn) // LANE)

    return pl.pallas_call(
        functools.partial(_w_prep_kernel, n_plain=n_plain, shift=SRC_COL_GATE - SRC_COL_DT),
        grid=(depth, D_MODEL // tr, P_COLS // tn),
        in_specs=[
            pl.BlockSpec((None, tr, tn), a_block),
            pl.BlockSpec((None, tr, LANE), b_block),
        ],
        out_specs=pl.BlockSpec((None, tr, tn), lambda d, i, j: (d, i, j)),
        out_shape=jax.ShapeDtypeStruct((depth, D_MODEL, P_COLS), BF16),
        compiler_params=_params(3),
        name="w_in_prep",
    )(w_in, w_in)


def _in_proj_kernel(x_ref, g_ref, w_ref, wdt_ref, dtb_ref, p_ref, dt_ref, u_ref, *, tn):
    j = pl.program_id(1)

    @pl.when(j == 0)
    def _():
        u = _rms_scale(x_ref[...], g_ref[...]).astype(BF16)
        u_ref[...] = u
        raw = _dot(u, wdt_ref[...]) + dtb_ref[...]
        dt_ref[...] = jnp.maximum(raw, 0.0) + jnp.log1p(jnp.exp(-jnp.abs(raw)))

    col = j * tn

    @pl.when(col < COL_Z)
    def _():
        p_ref[...] = _dot(u_ref[...], w_ref[...]).astype(BF16)

    @pl.when(col >= COL_Z)
    def _():
        acc = _dot(u_ref[...], w_ref[...])
        sig = _sigmoid(acc)
        p_ref[...] = jnp.where(col >= COL_G0, sig, acc * sig).astype(BF16)


def _in_proj(x, g, w, wdt, dtb, *, li, tm, tn):
    m = x.shape[0]
    return pl.pallas_call(
        functools.partial(_in_proj_kernel, tn=tn),
        grid=(m // tm, P_COLS // tn),
        in_specs=[
            pl.BlockSpec((tm, D_MODEL), lambda i, j: (i, 0)),
            pl.BlockSpec((1, D_MODEL), lambda i, j: (0, 0)),
            pl.BlockSpec((None, D_MODEL, tn), lambda i, j: (li, 0, j)),
            pl.BlockSpec((D_MODEL, DT_LANES), lambda i, j: (0, 0)),
            pl.BlockSpec((1, DT_LANES), lambda i, j: (0, 0)),
        ],
        out_specs=[
            pl.BlockSpec((tm, tn), lambda i, j: (i, j)),
            pl.BlockSpec((tm, DT_LANES), lambda i, j: (i, 0)),
        ],
        out_shape=[jax.ShapeDtypeStruct((m, P_COLS), BF16), jax.ShapeDtypeStruct((m, DT_LANES), F32)],
        scratch_shapes=[pltpu.VMEM((tm, D_MODEL), BF16)],
        compiler_params=_params(2),
        name="in_proj",
    )(x, g, w, wdt, dtb)


def _pool_kernel(p_ref, g0_ref, sth_ref, stl_ref, pw_ref, ps_ref, wpp_ref, o_ref, hh_ref, hl_ref, a_ref,
                 *, lt, pos0):
    ti = pl.program_id(1)

    @pl.when(ti == 0)
    def _():
        hh_ref[...] = sth_ref[0]
        hl_ref[...] = stl_ref[0]

    cur = p_ref[...]
    diff = lax.broadcasted_iota(jnp.int32, (lt, lt), 0) - lax.broadcasted_iota(jnp.int32, (lt, lt), 1)
    dist = (lax.broadcasted_iota(jnp.int32, (lt, POOL_HALO), 0) + POOL_HALO
            - lax.broadcasted_iota(jnp.int32, (lt, POOL_HALO), 1))
    pos = pos0 + ti * lt + lax.broadcasted_iota(jnp.int32, (lt, 1), 0)
    for g, w in enumerate(POOL_WINDOWS):
        cs = slice(g * POOL_GROUP, (g + 1) * POOL_GROUP)
        band = jnp.where(diff >= 0, jnp.where(diff < w, 1.0, 0.0), 0.0).astype(BF16)
        hband = jnp.where(dist < w, 1.0, 0.0).astype(BF16)
        xg = cur[:, cs]
        win = _dot(band, xg) + _dot(hband, hh_ref[:, cs]) + _dot(hband, hl_ref[:, cs])
        cnt = jnp.minimum(w, pos + 1).astype(F32)
        d = win * (1.0 / cnt) - xg.astype(F32)
        a_ref[:, cs] = (_dot(d.astype(BF16), pw_ref[g]) * ps_ref[:, cs]).astype(BF16)
    o_ref[...] = (_dot(a_ref[...], wpp_ref[...]) * g0_ref[...].astype(F32)).astype(BF16)
    hh_ref[...] = cur[lt - POOL_HALO:, :]
    hl_ref[...] = jnp.zeros_like(hl_ref)


def _pool_branch(pall, st_hi, st_lo, pool_w, pool_scale, wpp, *, li, b, l, lt, pos0):
    nt = l // lt
    m = b * l
    return pl.pallas_call(
        functools.partial(_pool_kernel, lt=lt, pos0=pos0),
        grid=(b, nt),
        in_specs=[
            pl.BlockSpec((lt, D_MODEL), lambda bi, ti: (bi * nt + ti, COL_POOL // D_MODEL)),
            pl.BlockSpec((lt, D_MODEL), lambda bi, ti: (bi * nt + ti, COL_G0 // D_MODEL)),
            pl.BlockSpec((1, POOL_HALO, D_MODEL), lambda bi, ti: (bi, 0, 0)),
            pl.BlockSpec((1, POOL_HALO, D_MODEL), lambda bi, ti: (bi, 0, 0)),
            pl.BlockSpec((None, len(POOL_WINDOWS), POOL_GROUP, POOL_GROUP), lambda bi, ti: (li, 0, 0, 0)),
            pl.BlockSpec((1, D_MODEL), lambda bi, ti: (0, 0)),
            pl.BlockSpec((None, D_MODEL, D_MODEL), lambda bi, ti: (li, 0, 0)),
        ],
        out_specs=pl.BlockSpec((lt, D_MODEL), lambda bi, ti: (bi * nt + ti, 0)),
        out_shape=jax.ShapeDtypeStruct((m, D_MODEL), BF16),
        scratch_shapes=[
            pltpu.VMEM((POOL_HALO, D_MODEL), BF16),
            pltpu.VMEM((POOL_HALO, D_MODEL), BF16),
            pltpu.VMEM((lt, D_MODEL), BF16),
        ],
        compiler_params=_params(2),
        name="pool_branch",
    )(pall, pall, st_hi, st_lo, pool_w, pool_scale, wpp)


def _split3(v):
    hi = v.astype(BF16)
    r1 = v - hi.astype(F32)
    mid = r1.astype(BF16)
    lo = (r1 - mid.astype(F32)).astype(BF16)
    return hi, mid, lo


def _ssd_kernel(xbc_ref, zs_ref, dt_ref, stc_ref, sts_ref, cw_ref, cb_ref, alog_ref, dsk_ref, nw_ref, sel_ref,
                o_ref, ns_ref, xc_ref, xa_ref, ht_ref, dtt_ref, cumt_ref, colb2_ref, yst2_ref, y2_ref, *, lv, nt):
    q = SSD_Q
    ti = pl.program_id(1)

    @pl.when(ti == 0)
    def _():
        xc_ref[0:CONV_HALO, :] = stc_ref[0]
        for g in range(N_GROUPS):
            sg = sts_ref[0, g * HEADS_PER_GROUP:(g + 1) * HEADS_PER_GROUP]
            ht_ref[g] = sg.reshape(GROUP_W, D_STATE).T

    if lv < q:
        xa_ref[lv:q, :] = jnp.zeros((q - lv, D_CONV), F32)
    cwid = 256

    def conv_body(c, carry):
        cs = pl.ds(pl.multiple_of(c * cwid, cwid), cwid)
        xc_ref[CONV_HALO:CONV_HALO + lv, cs] = xbc_ref[:, cs].astype(F32)
        full = xc_ref[0:CONV_HALO + lv, cs]
        acc = full[CONV_HALO:] * cw_ref[CONV_W - 1:CONV_W, cs] + cb_ref[:, cs]
        for k in range(1, CONV_W):
            tap = CONV_W - 1 - k
            acc = acc + pltpu.roll(full, k, axis=0)[CONV_HALO:] * cw_ref[tap:tap + 1, cs]
        xa_ref[0:lv, cs] = acc * _sigmoid(acc)
        return carry

    lax.fori_loop(0, D_CONV // cwid, conv_body, 0)
    xc_ref[0:CONV_HALO, :] = xc_ref[lv:lv + CONV_HALO, :]

    if lv < q:
        dt_tm = jnp.concatenate([dt_ref[...], jnp.zeros((q - lv, DT_LANES), F32)], axis=0)
    else:
        dt_tm = dt_ref[...]
    a2 = dt_tm * (-LOG2E * jnp.exp(alog_ref[...]))
    row_i = lax.broadcasted_iota(jnp.int32, (q, q), 0)
    col_i = lax.broadcasted_iota(jnp.int32, (q, q), 1)
    causal = row_i >= col_i
    lower = jnp.where(causal, 1.0, 0.0).astype(BF16)
    hi, mid, lo = _split3(a2)
    cum = _dot(lower, hi) + _dot(lower, mid) + _dot(lower, lo)
    c2 = jnp.concatenate(_split3(cum)[0:2], axis=1)
    cumt_ref[...] = cum.T
    dtt_ref[...] = dt_tm.T
    lane_row = lax.broadcasted_iota(jnp.int32, (1, LANE), 1)
    half0 = jnp.where(lane_row < HEAD_DIM, 1.0, 0.0).astype(BF16)
    half1 = jnp.where(lane_row >= HEAD_DIM, 1.0, 0.0).astype(BF16)

    def one_group(g, slot):
        colb_ref, yst_ref, y_ref = colb2_ref.at[slot], yst2_ref.at[slot], y2_ref.at[slot]
        go = g * GROUP_W
        bg = xa_ref[:, pl.ds(pl.multiple_of(D_INNER + g * D_STATE, D_STATE), D_STATE)]
        cg = xa_ref[:, pl.ds(pl.multiple_of(D_INNER + N_GROUPS * D_STATE + g * D_STATE, D_STATE), D_STATE)]
        cg16 = cg.astype(BF16)
        cb = lax.dot_general(cg16, bg.astype(BF16), (((1,), (1,)), ((), ())), preferred_element_type=F32)
        bt = bg.T
        selg = sel_ref[pl.ds(pl.multiple_of(SEL_ROW0 - HEAD_LANE_STRIDE * g, HEAD_LANE_STRIDE), DT_LANES), :]
        colb_ref[...] = _dot(c2, jnp.concatenate([selg, selg], axis=0))
        yst_ref[...] = _dot(cg16, ht_ref[g].astype(BF16))
        for j in range(HEADS_PER_GROUP // 2):
            pair = slice(j * LANE, (j + 1) * LANE)
            mixes = []
            bws = []
            lasts = []
            for e in range(2):
                r = 2 * j + e
                h = g * HEAD_LANE_STRIDE + r
                cum_row = cumt_ref[pl.ds(h, 1), :]
                last = cum_row[:, q - 1:q]
                src_row = cum_row - jnp.log2(dtt_ref[pl.ds(h, 1), :])
                cum_col = colb_ref[:, r * LANE:(r + 1) * LANE]
                mixes.append((cb * jnp.exp2(jnp.where(causal, cum_col - src_row, NEG_BIG))).astype(BF16))
                bws.append((bt * jnp.exp2(last - src_row)).astype(BF16))
                lasts.append(jnp.exp2(last))
            xs = xa_ref[:, pl.ds(pl.multiple_of(go + j * LANE, LANE), LANE)]
            xs16 = xs.astype(BF16)
            rhs = jnp.concatenate([xs16 * half0, xs16 * half1], axis=0)
            dsk = dsk_ref[:, pl.ds(pl.multiple_of(go + j * LANE, LANE), LANE)]
            cum_pair = colb_ref[:, (HEADS_PER_GROUP + j) * LANE:(HEADS_PER_GROUP + j + 1) * LANE]
            y_ref[:, pair] = (_dot(jnp.concatenate(mixes, axis=1), rhs)
                              + yst_ref[:, pair] * jnp.exp2(cum_pair) + dsk * xs)
            keep = jnp.where(lane_row < HEAD_DIM, lasts[0], lasts[1])
            ht_ref[g, :, pair] = ht_ref[g, :, pair] * keep + _dot(jnp.concatenate(bws, axis=1), rhs)
        gs = pl.ds(pl.multiple_of(go, GROUP_W), GROUP_W)
        gated = y_ref[0:lv, :] * zs_ref[:, gs].astype(F32)
        o_ref[:, gs] = _rms_scale(gated, nw_ref[:, gs]).astype(BF16)

    def group_batch_body(i, carry):
        for slot in range(GROUP_SLOTS):
            one_group(GROUP_SLOTS * i + slot, slot)
        return carry

    lax.fori_loop(0, N_GROUPS // GROUP_SLOTS, group_batch_body, 0)

    @pl.when(ti == nt - 1)
    def _():
        for g in range(N_GROUPS):
            ns_ref[0, g * HEADS_PER_GROUP:(g + 1) * HEADS_PER_GROUP] = (
                ht_ref[g].T.reshape(HEADS_PER_GROUP, HEAD_DIM, D_STATE))


def _ssd_branch(pall, dt, st_conv, st_ssm, conv_w, conv_b, alog_row, dskip_row, norm_w, sel, *, ls, b, l):
    lv = min(l, SSD_Q)
    nt = l // lv
    m = b * l
    return pl.pallas_call(
        functools.partial(_ssd_kernel, lv=lv, nt=nt),
        grid=(b, nt),
        in_specs=[
            pl.BlockSpec((lv, D_CONV), lambda bi, ti: (bi * nt + ti, COL_XBC // D_CONV)),
            pl.BlockSpec((lv, D_INNER), lambda bi, ti: (bi * nt + ti, COL_Z // D_INNER)),
            pl.BlockSpec((lv, DT_LANES), lambda bi, ti: (bi * nt + ti, 0)),
            pl.BlockSpec((1, CONV_HALO, D_CONV), lambda bi, ti: (bi, 0, 0)),
            pl.BlockSpec((None, 1, N_HEADS, HEAD_DIM, D_STATE), lambda bi, ti: (ls, bi, 0, 0, 0)),
            pl.BlockSpec((CONV_W, D_CONV), lambda bi, ti: (0, 0)),
            pl.BlockSpec((1, D_CONV), lambda bi, ti: (0, 0)),
            pl.BlockSpec((1, DT_LANES), lambda bi, ti: (0, 0)),
            pl.BlockSpec((1, D_INNER), lambda bi, ti: (0, 0)),
            pl.BlockSpec((1, D_INNER), lambda bi, ti: (0, 0)),
            pl.BlockSpec((SEL_ROWS, SEL_COLS), lambda bi, ti: (0, 0)),
        ],
        out_specs=[
            pl.BlockSpec((lv, D_INNER), lambda bi, ti: (bi * nt + ti, 0)),
            pl.BlockSpec((1, N_HEADS, HEAD_DIM, D_STATE), lambda bi, ti: (bi, 0, 0, 0)),
        ],
        out_shape=[
            jax.ShapeDtypeStruct((m, D_INNER), BF16),
            jax.ShapeDtypeStruct((b, N_HEADS, HEAD_DIM, D_STATE), F32),
        ],
        scratch_shapes=[
            pltpu.VMEM((CONV_HALO + SSD_Q, D_CONV), F32),
            pltpu.VMEM((SSD_Q, D_CONV), F32),
            pltpu.VMEM((N_GROUPS, D_STATE, GROUP_W), F32),
            pltpu.VMEM((DT_LANES, SSD_Q), F32),
            pltpu.VMEM((DT_LANES, SSD_Q), F32),
            pltpu.VMEM((GROUP_SLOTS, SSD_Q, SEL_COLS), F32),
            pltpu.VMEM((GROUP_SLOTS, SSD_Q, GROUP_W), F32),
            pltpu.VMEM((GROUP_SLOTS, SSD_Q, GROUP_W), F32),
        ],
        compiler_params=_params(2),
        name="ssd_branch",
    )(pall, pall, dt, st_conv, st_ssm, conv_w, conv_b, alog_row, dskip_row, norm_w, sel)


def _mix_out_kernel(a_ref, ws_ref, bp_ref, g1_ref, wo_ref, gp_ref, x_ref, o_ref):
    br_ssd = _dot(a_ref[...], ws_ref[...])
    merged = (bp_ref[...].astype(F32) + g1_ref[...].astype(F32) * br_ssd).astype(BF16)
    out = _dot(merged, wo_ref[...])
    o_ref[...] = x_ref[...] + _rms_scale(out, gp_ref[...])


def _mix_out(a_ssd, w_ssd, bp, pall, w_out, g_post, x, *, li, tm):
    m = x.shape[0]
    resident = pl.Buffered(1)
    return pl.pallas_call(
        _mix_out_kernel,
        grid=(m // tm,),
        in_specs=[
            pl.BlockSpec((tm, D_INNER), lambda i: (i, 0)),
            pl.BlockSpec((None, D_INNER, D_MODEL), lambda i: (li, 0, 0), pipeline_mode=resident),
            pl.BlockSpec((tm, D_MODEL), lambda i: (i, 0)),
            pl.BlockSpec((tm, D_MODEL), lambda i: (i, COL_G1 // D_MODEL)),
            pl.BlockSpec((None, D_MODEL, D_MODEL), lambda i: (li, 0, 0), pipeline_mode=resident),
            pl.BlockSpec((1, D_MODEL), lambda i: (0, 0)),
            pl.BlockSpec((tm, D_MODEL), lambda i: (i, 0)),
        ],
        out_specs=pl.BlockSpec((tm, D_MODEL), lambda i: (i, 0)),
        out_shape=jax.ShapeDtypeStruct((m, D_MODEL), F32),
        compiler_params=_params(1),
        name="mix_out",
    )(a_ssd, w_ssd, bp, pall, w_out, g_post, x)


def _mlp_kernel(x_ref, gpre_ref, wu_ref, wd_ref, gpost_ref, o_ref, u_ref, acc_ref, *, nf):
    j = pl.program_id(1)

    @pl.when(j == 0)
    def _():
        u_ref[...] = _rms_scale(x_ref[...], gpre_ref[...]).astype(BF16)
        acc_ref[...] = jnp.zeros_like(acc_ref)

    hid = jnp.maximum(_dot(u_ref[...], wu_ref[...]), 0.0)
    acc_ref[...] += _dot((hid * hid).astype(BF16), wd_ref[...])

    @pl.when(j == nf - 1)
    def _():
        o_ref[...] = x_ref[...] + _rms_scale(acc_ref[...], gpost_ref[...])


def _mlp(x, g_pre, w_up, w_down, g_post, *, li, tm, tf):
    m = x.shape[0]
    nf = D_FF // tf
    return pl.pallas_call(
        functools.partial(_mlp_kernel, nf=nf),
        grid=(m // tm, nf),
        in_specs=[
            pl.BlockSpec((tm, D_MODEL), lambda i, j: (i, 0)),
            pl.BlockSpec((1, D_MODEL), lambda i, j: (0, 0)),
            pl.BlockSpec((None, D_MODEL, tf), lambda i, j: (li, 0, j)),
            pl.BlockSpec((None, tf, D_MODEL), lambda i, j: (li, j, 0)),
            pl.BlockSpec((1, D_MODEL), lambda i, j: (0, 0)),
        ],
        out_specs=pl.BlockSpec((tm, D_MODEL), lambda i, j: (i, 0)),
        out_shape=jax.ShapeDtypeStruct((m, D_MODEL), F32),
        scratch_shapes=[pltpu.VMEM((tm, D_MODEL), BF16), pltpu.VMEM((tm, D_MODEL), F32)],
        compiler_params=_params(2),
        name="mlp",
    )(x, g_pre, w_up, w_down, g_post)


def _tiles(m):
    return min(m, 512)


def _layer_group(x, st_pool, st_conv, st_ssm, pos0, p, big, *, li, ls, b, l):
    m = b * l
    tm = _tiles(m)
    pall, dt = _in_proj(x, p["g_mix_pre"], big["w_in"], p["w_dt"], p["dt_bias"], li=li, tm=tm, tn=2048)

    st16 = jnp.pad(st_pool, ((0, 0), (POOL_HALO - POOL_KEEP, 0), (0, 0)))
    st_hi = st16.astype(BF16)
    st_lo = (st16 - st_hi.astype(F32)).astype(BF16)
    bp = _pool_branch(pall, st_hi, st_lo, big["pool_w"], p["pool_scale"], big["w_pool_proj"],
                      li=li, b=b, l=l, lt=min(l, 256), pos0=pos0)

    stc = jnp.pad(st_conv, ((0, 0), (CONV_HALO - (CONV_W - 1), 0), (0, 0)))
    a_ssd, new_ssm = _ssd_branch(pall, dt, stc, st_ssm, p["conv_w"], p["conv_b"], p["a_log"], p["d_skip"],
                                 p["ssd_norm"], _lane_selector(), ls=ls, b=b, l=l)

    x = _mix_out(a_ssd, big["w_ssd_proj"], bp, pall, big["w_out"], p["g_mix_post"], x, li=li, tm=min(m, 256))
    x = _mlp(x, p["g_mlp_pre"], big["w_up"], big["w_down"], p["g_mlp_post"], li=li, tm=tm, tf=1024)

    p3 = pall.reshape(b, l, P_COLS)
    new_pool = p3[:, l - POOL_KEEP:, COL_POOL:COL_POOL + D_MODEL].astype(F32)
    new_conv = p3[:, l - (CONV_W - 1):, COL_XBC:COL_XBC + D_CONV].astype(F32)
    return x, new_pool, new_conv, new_ssm


def _heads_to_lanes(v):
    lead = v.shape[:-1]
    v = v.reshape(lead + (N_GROUPS, HEADS_PER_GROUP))
    v = jnp.pad(v, [(0, 0)] * len(lead) + [(0, 0), (0, HEAD_LANE_STRIDE - HEADS_PER_GROUP)])
    return v.reshape(lead + (DT_LANES,))


def _lane_selector():
    j = lax.broadcasted_iota(jnp.int32, (SEL_ROWS, SEL_COLS), 0)
    c = lax.broadcasted_iota(jnp.int32, (SEL_ROWS, SEL_COLS), 1)
    blk = c // LANE
    src = jnp.where(blk < HEADS_PER_GROUP, blk, 2 * (blk - HEADS_PER_GROUP) + (c % LANE) // HEAD_DIM)
    return jnp.where(j == SEL_ROW0 + src, 1.0, 0.0).astype(BF16)


def _prep_big(w_in, pool_w, w_pool_proj, w_ssd_proj, w_out, w_up, w_down):
    return {
        "w_in": _prep_w_in(w_in),
        "pool_w": pool_w.astype(BF16),
        "w_pool_proj": w_pool_proj.astype(BF16),
        "w_ssd_proj": w_ssd_proj.astype(BF16),
        "w_out": w_out.astype(BF16),
        "w_up": w_up.astype(BF16),
        "w_down": w_down.astype(BF16),
    }


def _prep_layer(w_in, pool_scale, conv_w, conv_b, dt_bias, a_log, d_skip, ssd_norm, g_mix_pre, g_mix_post,
                g_mlp_pre, g_mlp_post):
    w_dt = _heads_to_lanes(w_in[:, SRC_COL_DT:SRC_COL_GATE]).astype(BF16)
    row = lambda v: v.reshape(1, -1)
    return {
        "w_dt": w_dt,
        "dt_bias": _heads_to_lanes(row(dt_bias)),
        "pool_scale": row(pool_scale),
        "conv_w": conv_w,
        "conv_b": row(conv_b),
        "a_log": _heads_to_lanes(row(a_log)),
        "d_skip": row(jnp.repeat(d_skip, HEAD_DIM)),
        "ssd_norm": row(ssd_norm),
        "g_mix_pre": row(g_mix_pre),
        "g_mix_post": row(g_mix_post),
        "g_mlp_pre": row(g_mlp_pre),
        "g_mlp_post": row(g_mlp_post),
    }


@jax.jit
def kernel(x_prompt, x_sample, state_pool, state_conv, state_ssm, w_in, pool_w, pool_scale, conv_w, conv_b,
           dt_bias, a_log, d_skip, ssd_norm, w_pool_proj, w_ssd_proj, w_out, g_mix_pre, g_mix_post,
           w_up, w_down, g_mlp_pre, g_mlp_post):
    bp, lp, _ = x_prompt.shape
    bs, ls, _ = x_sample.shape
    depth = w_in.shape[0]
    past_len = 1024
    zero_pool = jnp.zeros((bp, POOL_KEEP, D_MODEL), F32)
    zero_conv = jnp.zeros((bp, CONV_W - 1, D_CONV), F32)
    zero_ssm = jnp.zeros((1, bp, N_HEADS, HEAD_DIM, D_STATE), F32)
    yp = x_prompt.reshape(bp * lp, D_MODEL)
    ys = x_sample.reshape(bs * ls, D_MODEL)
    big = _prep_big(w_in, pool_w, w_pool_proj, w_ssd_proj, w_out, w_up, w_down)
    outs = [[] for _ in range(6)]
    for li in range(depth):
        p = _prep_layer(w_in[li], pool_scale[li], conv_w[li], conv_b[li], dt_bias[li], a_log[li], d_skip[li],
                        ssd_norm[li], g_mix_pre[li], g_mix_post[li], g_mlp_pre[li], g_mlp_post[li])
        yp, pool_p, conv_p, ssm_p = _layer_group(yp, zero_pool, zero_conv, zero_ssm, 0, p, big,
                                                 li=li, ls=0, b=bp, l=lp)
        ys, pool_s, conv_s, ssm_s = _layer_group(ys, state_pool[li], state_conv[li], state_ssm, past_len, p, big,
                                                 li=li, ls=li, b=bs, l=ls)
        for lst, v in zip(outs, (pool_p, conv_p, ssm_p, pool_s, conv_s, ssm_s)):
            lst.append(v)
    return (yp.reshape(bp, lp, D_MODEL), ys.reshape(bs, ls, D_MODEL)) + tuple(jnp.stack(v) for v in outs)
```

```python
import functools

import jax
import jax.numpy as jnp
from jax import lax
from jax.experimental import pallas as pl
from jax.experimental.pallas import tpu as pltpu

F32 = jnp.float32
BF16 = jnp.bfloat16

D_MODEL = 2048
EPS = 1e-6
POOL_WINDOWS = (2, 4, 8, 16)
POOL_GROUP = D_MODEL // len(POOL_WINDOWS)
POOL_KEEP = max(POOL_WINDOWS) - 1
POOL_HALO = 16
D_INNER = 2 * D_MODEL
HEAD_DIM = 64
N_HEADS = D_INNER // HEAD_DIM
N_GROUPS = 8
HEADS_PER_GROUP = N_HEADS // N_GROUPS
GROUP_W = D_INNER // N_GROUPS
D_STATE = 128
CONV_W = 4
D_CONV = D_INNER + 2 * N_GROUPS * D_STATE
SSD_Q = 128
GROUP_SLOTS = 4
CONV_HALO = 8
D_FF = 4 * D_MODEL
COL_XBC = 0
COL_POOL = COL_XBC + D_CONV
COL_Z = COL_POOL + D_MODEL
COL_G0 = COL_Z + D_INNER
COL_G1 = COL_G0 + D_MODEL
P_COLS = COL_G1 + D_MODEL
SRC_COL_POOL = 0
SRC_COL_XBC = SRC_COL_POOL + D_MODEL + D_INNER
SRC_COL_DT = SRC_COL_XBC + D_CONV
SRC_COL_GATE = SRC_COL_DT + N_HEADS
IN_COLS = SRC_COL_GATE + 2 * D_MODEL
DT_LANES = 128
HEAD_LANE_STRIDE = DT_LANES // N_GROUPS
SEL_ROW0 = DT_LANES - HEAD_LANE_STRIDE
SEL_ROWS = SEL_ROW0 + DT_LANES
SEL_COLS = (HEADS_PER_GROUP + HEADS_PER_GROUP // 2) * 128
NEG_BIG = -1e30
LOG2E = 1.4426950408889634
LANE = 128
VMEM_LIMIT = 56 * 1024 * 1024


def _params(n_axes):
    return pltpu.CompilerParams(dimension_semantics=("arbitrary",) * n_axes, vmem_limit_bytes=VMEM_LIMIT)


def _sigmoid(v):
    return 1.0 / (1.0 + jnp.exp2(v * (-LOG2E)))


def _rms_scale(v, g):
    ms = jnp.mean(v * v, axis=-1, keepdims=True)
    return v * lax.rsqrt(ms + EPS) * g


def _dot(a, b):
    return jnp.dot(a, b, preferred_element_type=F32)


def _dot_nt(a, b):
    return lax.dot_general(a, b, (((1,), (1,)), ((), ())), preferred_element_type=F32)


def _in_proj_kernel(x_ref, g_ref, w_ref, wdt_ref, dtb_ref, p_ref, dt_ref, u_ref, *, tn):
    j = pl.program_id(1)

    @pl.when(j == 0)
    def _():
        u = _rms_scale(x_ref[...], g_ref[...]).astype(BF16)
        u_ref[...] = u
        raw = _dot_nt(u, wdt_ref[...]) + dtb_ref[...]
        dt_ref[...] = jnp.maximum(raw, 0.0) + jnp.log1p(jnp.exp(-jnp.abs(raw)))

    col = j * tn

    @pl.when(col < COL_Z)
    def _():
        p_ref[...] = _dot_nt(u_ref[...], w_ref[0]).astype(BF16)

    @pl.when(col >= COL_Z)
    def _():
        acc = _dot_nt(u_ref[...], w_ref[0])
        sig = _sigmoid(acc)
        p_ref[...] = jnp.where(col >= COL_G0, sig, acc * sig).astype(BF16)


def _in_proj(x, g, wt, wdt, dtb, *, li, tm, tn):
    m = x.shape[0]
    n_x = D_CONV // tn
    n_plain = n_x + (D_MODEL + D_INNER) // tn

    unit = N_HEADS

    def w_rows(i, j):
        plain = jnp.where(j < n_x, (SRC_COL_XBC + j * tn) // unit, (SRC_COL_POOL + (j - n_x) * tn) // unit)
        return (li, jnp.where(j < n_plain, plain, (SRC_COL_GATE + (j - n_plain) * tn) // unit) * unit, 0)

    return pl.pallas_call(
        functools.partial(_in_proj_kernel, tn=tn),
        grid=(m // tm, P_COLS // tn),
        in_specs=[
            pl.BlockSpec((tm, D_MODEL), lambda i, j: (i, 0)),
            pl.BlockSpec((1, D_MODEL), lambda i, j: (0, 0)),
            pl.BlockSpec((pl.Element(1), pl.Element(tn), pl.Element(D_MODEL)), w_rows),
            pl.BlockSpec((DT_LANES, D_MODEL), lambda i, j: (0, 0)),
            pl.BlockSpec((1, DT_LANES), lambda i, j: (0, 0)),
        ],
        out_specs=[
            pl.BlockSpec((tm, tn), lambda i, j: (i, j)),
            pl.BlockSpec((tm, DT_LANES), lambda i, j: (i, 0)),
        ],
        out_shape=[jax.ShapeDtypeStruct((m, P_COLS), BF16), jax.ShapeDtypeStruct((m, DT_LANES), F32)],
        scratch_shapes=[pltpu.VMEM((tm, D_MODEL), BF16)],
        compiler_params=_params(2),
        name="in_proj",
    )(x, g, wt, wdt, dtb)


def _pool_kernel(p_ref, g0_ref, sth_ref, stl_ref, pw_ref, ps_ref, wpp_ref, o_ref, hh_ref, hl_ref, a_ref,
                 *, lt, pos0):
    ti = pl.program_id(1)

    @pl.when(ti == 0)
    def _():
        hh_ref[...] = sth_ref[0]
        hl_ref[...] = stl_ref[0]

    cur = p_ref[...]
    diff = lax.broadcasted_iota(jnp.int32, (lt, lt), 0) - lax.broadcasted_iota(jnp.int32, (lt, lt), 1)
    dist = (lax.broadcasted_iota(jnp.int32, (lt, POOL_HALO), 0) + POOL_HALO
            - lax.broadcasted_iota(jnp.int32, (lt, POOL_HALO), 1))
    pos = pos0 + ti * lt + lax.broadcasted_iota(jnp.int32, (lt, 1), 0)
    for g, w in enumerate(POOL_WINDOWS):
        cs = slice(g * POOL_GROUP, (g + 1) * POOL_GROUP)
        band = jnp.where(diff >= 0, jnp.where(diff < w, 1.0, 0.0), 0.0).astype(BF16)
        hband = jnp.where(dist < w, 1.0, 0.0).astype(BF16)
        xg = cur[:, cs]
        win = _dot(band, xg) + _dot(hband, hh_ref[:, cs]) + _dot(hband, hl_ref[:, cs])
        cnt = jnp.minimum(w, pos + 1).astype(F32)
        d = win * (1.0 / cnt) - xg.astype(F32)
        a_ref[:, cs] = (_dot(d.astype(BF16), pw_ref[g]) * ps_ref[:, cs]).astype(BF16)
    o_ref[...] = (_dot(a_ref[...], wpp_ref[...]) * g0_ref[...].astype(F32)).astype(BF16)
    hh_ref[...] = cur[lt - POOL_HALO:, :]
    hl_ref[...] = jnp.zeros_like(hl_ref)


def _pool_branch(pall, st_hi, st_lo, pool_w, pool_scale, wpp, *, li, b, l, lt, pos0):
    nt = l // lt
    m = b * l
    return pl.pallas_call(
        functools.partial(_pool_kernel, lt=lt, pos0=pos0),
        grid=(b, nt),
        in_specs=[
            pl.BlockSpec((lt, D_MODEL), lambda bi, ti: (bi * nt + ti, COL_POOL // D_MODEL)),
            pl.BlockSpec((lt, D_MODEL), lambda bi, ti: (bi * nt + ti, COL_G0 // D_MODEL)),
            pl.BlockSpec((1, POOL_HALO, D_MODEL), lambda bi, ti: (bi, 0, 0)),
            pl.BlockSpec((1, POOL_HALO, D_MODEL), lambda bi, ti: (bi, 0, 0)),
            pl.BlockSpec((None, len(POOL_WINDOWS), POOL_GROUP, POOL_GROUP), lambda bi, ti: (li, 0, 0, 0)),
            pl.BlockSpec((1, D_MODEL), lambda bi, ti: (0, 0)),
            pl.BlockSpec((None, D_MODEL, D_MODEL), lambda bi, ti: (li, 0, 0)),
        ],
        out_specs=pl.BlockSpec((lt, D_MODEL), lambda bi, ti: (bi * nt + ti, 0)),
        out_shape=jax.ShapeDtypeStruct((m, D_MODEL), BF16),
        scratch_shapes=[
            pltpu.VMEM((POOL_HALO, D_MODEL), BF16),
            pltpu.VMEM((POOL_HALO, D_MODEL), BF16),
            pltpu.VMEM((lt, D_MODEL), BF16),
        ],
        compiler_params=_params(2),
        name="pool_branch",
    )(pall, pall, st_hi, st_lo, pool_w, pool_scale, wpp)


def _split3(v):
    hi = v.astype(BF16)
    r1 = v - hi.astype(F32)
    mid = r1.astype(BF16)
    lo = (r1 - mid.astype(F32)).astype(BF16)
    return hi, mid, lo


def _ssd_kernel(xbc_ref, zs_ref, dt_ref, stc_ref, sts_ref, cw_ref, cb_ref, alog_ref, dsk_ref, nw_ref, sel_ref,
                o_ref, ns_ref, xc_ref, xa_ref, ht_ref, dtt_ref, cumt_ref, colb2_ref, yst2_ref, y2_ref, *, lv, nt):
    q = SSD_Q
    ti = pl.program_id(1)

    @pl.when(ti == 0)
    def _():
        xc_ref[0:CONV_HALO, :] = stc_ref[0]
        for g in range(N_GROUPS):
            sg = sts_ref[0, g * HEADS_PER_GROUP:(g + 1) * HEADS_PER_GROUP]
            ht_ref[g] = sg.reshape(GROUP_W, D_STATE).T

    if lv < q:
        xa_ref[lv:q, :] = jnp.zeros((q - lv, D_CONV), F32)
    cwid = 256

    def conv_body(c, carry):
        cs = pl.ds(pl.multiple_of(c * cwid, cwid), cwid)
        xc_ref[CONV_HALO:CONV_HALO + lv, cs] = xbc_ref[:, cs].astype(F32)
        full = xc_ref[0:CONV_HALO + lv, cs]
        acc = full[CONV_HALO:] * cw_ref[CONV_W - 1:CONV_W, cs] + cb_ref[:, cs]
        for k in range(1, CONV_W):
            tap = CONV_W - 1 - k
            acc = acc + pltpu.roll(full, k, axis=0)[CONV_HALO:] * cw_ref[tap:tap + 1, cs]
        xa_ref[0:lv, cs] = acc * _sigmoid(acc)
        return carry

    lax.fori_loop(0, D_CONV // cwid, conv_body, 0)
    xc_ref[0:CONV_HALO, :] = xc_ref[lv:lv + CONV_HALO, :]

    if lv < q:
        dt_tm = jnp.concatenate([dt_ref[...], jnp.zeros((q - lv, DT_LANES), F32)], axis=0)
    else:
        dt_tm = dt_ref[...]
    a2 = dt_tm * (-LOG2E * jnp.exp(alog_ref[...]))
    row_i = lax.broadcasted_iota(jnp.int32, (q, q), 0)
    col_i = lax.broadcasted_iota(jnp.int32, (q, q), 1)
    causal = row_i >= col_i
    lower = jnp.where(causal, 1.0, 0.0).astype(BF16)
    hi, mid, lo = _split3(a2)
    cum = _dot(lower, hi) + _dot(lower, mid) + _dot(lower, lo)
    c2 = jnp.concatenate(_split3(cum)[0:2], axis=1)
    cumt_ref[...] = cum.T
    dtt_ref[...] = dt_tm.T
    lane_row = lax.broadcasted_iota(jnp.int32, (1, LANE), 1)
    half0 = jnp.where(lane_row < HEAD_DIM, 1.0, 0.0).astype(BF16)
    half1 = jnp.where(lane_row >= HEAD_DIM, 1.0, 0.0).astype(BF16)

    def one_group(g, slot):
        colb_ref, yst_ref, y_ref = colb2_ref.at[slot], yst2_ref.at[slot], y2_ref.at[slot]
        go = g * GROUP_W
        bg = xa_ref[:, pl.ds(pl.multiple_of(D_INNER + g * D_STATE, D_STATE), D_STATE)]
        cg = xa_ref[:, pl.ds(pl.multiple_of(D_INNER + N_GROUPS * D_STATE + g * D_STATE, D_STATE), D_STATE)]
        cg16 = cg.astype(BF16)
        cb = lax.dot_general(cg16, bg.astype(BF16), (((1,), (1,)), ((), ())), preferred_element_type=F32)
        bt = bg.T
        selg = sel_ref[pl.ds(pl.multiple_of(SEL_ROW0 - HEAD_LANE_STRIDE * g, HEAD_LANE_STRIDE), DT_LANES), :]
        colb_ref[...] = _dot(c2, jnp.concatenate([selg, selg], axis=0))
        yst_ref[...] = _dot(cg16, ht_ref[g].astype(BF16))
        for j in range(HEADS_PER_GROUP // 2):
            pair = slice(j * LANE, (j + 1) * LANE)
            mixes = []
            bws = []
            lasts = []
            for e in range(2):
                r = 2 * j + e
                h = g * HEAD_LANE_STRIDE + r
                cum_row = cumt_ref[pl.ds(h, 1), :]
                last = cum_row[:, q - 1:q]
                src_row = cum_row - jnp.log2(dtt_ref[pl.ds(h, 1), :])
                cum_col = colb_ref[:, r * LANE:(r + 1) * LANE]
                mixes.append((cb * jnp.exp2(jnp.where(causal, cum_col - src_row, NEG_BIG))).astype(BF16))
                bws.append((bt * jnp.exp2(last - src_row)).astype(BF16))
                lasts.append(jnp.exp2(last))
            xs = xa_ref[:, pl.ds(pl.multiple_of(go + j * LANE, LANE), LANE)]
            xs16 = xs.astype(BF16)
            rhs = jnp.concatenate([xs16 * half0, xs16 * half1], axis=0)
            dsk = dsk_ref[:, pl.ds(pl.multiple_of(go + j * LANE, LANE), LANE)]
            cum_pair = colb_ref[:, (HEADS_PER_GROUP + j) * LANE:(HEADS_PER_GROUP + j + 1) * LANE]
            y_ref[:, pair] = (_dot(jnp.concatenate(mixes, axis=1), rhs)
                              + yst_ref[:, pair] * jnp.exp2(cum_pair) + dsk * xs)
            keep = jnp.where(lane_row < HEAD_DIM, lasts[0], lasts[1])
            ht_ref[g, :, pair] = ht_ref[g, :, pair] * keep + _dot(jnp.concatenate(bws, axis=1), rhs)
        gs = pl.ds(pl.multiple_of(go, GROUP_W), GROUP_W)
        gated = y_ref[0:lv, :] * zs_ref[:, gs].astype(F32)
        o_ref[:, gs] = _rms_scale(gated, nw_ref[:, gs]).astype(BF16)

    def group_batch_body(i, carry):
        for slot in range(GROUP_SLOTS):
            one_group(GROUP_SLOTS * i + slot, slot)
        return carry

    lax.fori_loop(0, N_GROUPS // GROUP_SLOTS, group_batch_body, 0)

    @pl.when(ti == nt - 1)
    def _():
        for g in range(N_GROUPS):
            ns_ref[0, g * HEADS_PER_GROUP:(g + 1) * HEADS_PER_GROUP] = (
                ht_ref[g].T.reshape(HEADS_PER_GROUP, HEAD_DIM, D_STATE))


def _ssd_branch(pall, dt, st_conv, st_ssm, conv_w, conv_b, alog_row, dskip_row, norm_w, sel, *, ls, b, l):
    lv = min(l, SSD_Q)
    nt = l // lv
    m = b * l
    return pl.pallas_call(
        functools.partial(_ssd_kernel, lv=lv, nt=nt),
        grid=(b, nt),
        in_specs=[
            pl.BlockSpec((lv, D_CONV), lambda bi, ti: (bi * nt + ti, COL_XBC // D_CONV)),
            pl.BlockSpec((lv, D_INNER), lambda bi, ti: (bi * nt + ti, COL_Z // D_INNER)),
            pl.BlockSpec((lv, DT_LANES), lambda bi, ti: (bi * nt + ti, 0)),
            pl.BlockSpec((1, CONV_HALO, D_CONV), lambda bi, ti: (bi, 0, 0)),
            pl.BlockSpec((None, 1, N_HEADS, HEAD_DIM, D_STATE), lambda bi, ti: (ls, bi, 0, 0, 0)),
            pl.BlockSpec((CONV_W, D_CONV), lambda bi, ti: (0, 0)),
            pl.BlockSpec((1, D_CONV), lambda bi, ti: (0, 0)),
            pl.BlockSpec((1, DT_LANES), lambda bi, ti: (0, 0)),
            pl.BlockSpec((1, D_INNER), lambda bi, ti: (0, 0)),
            pl.BlockSpec((1, D_INNER), lambda bi, ti: (0, 0)),
            pl.BlockSpec((SEL_ROWS, SEL_COLS), lambda bi, ti: (0, 0)),
        ],
        out_specs=[
            pl.BlockSpec((lv, D_INNER), lambda bi, ti: (bi * nt + ti, 0)),
            pl.BlockSpec((1, N_HEADS, HEAD_DIM, D_STATE), lambda bi, ti: (bi, 0, 0, 0)),
        ],
        out_shape=[
            jax.ShapeDtypeStruct((m, D_INNER), BF16),
            jax.ShapeDtypeStruct((b, N_HEADS, HEAD_DIM, D_STATE), F32),
        ],
        scratch_shapes=[
            pltpu.VMEM((CONV_HALO + SSD_Q, D_CONV), F32),
            pltpu.VMEM((SSD_Q, D_CONV), F32),
            pltpu.VMEM((N_GROUPS, D_STATE, GROUP_W), F32),
            pltpu.VMEM((DT_LANES, SSD_Q), F32),
            pltpu.VMEM((DT_LANES, SSD_Q), F32),
            pltpu.VMEM((GROUP_SLOTS, SSD_Q, SEL_COLS), F32),
            pltpu.VMEM((GROUP_SLOTS, SSD_Q, GROUP_W), F32),
            pltpu.VMEM((GROUP_SLOTS, SSD_Q, GROUP_W), F32),
        ],
        compiler_params=_params(2),
        name="ssd_branch",
    )(pall, pall, dt, st_conv, st_ssm, conv_w, conv_b, alog_row, dskip_row, norm_w, sel)


def _mix_out_kernel(a_ref, ws_ref, bp_ref, g1_ref, wo_ref, gp_ref, x_ref, o_ref):
    br_ssd = _dot(a_ref[...], ws_ref[...])
    merged = (bp_ref[...].astype(F32) + g1_ref[...].astype(F32) * br_ssd).astype(BF16)
    out = _dot(merged, wo_ref[...])
    o_ref[...] = x_ref[...] + _rms_scale(out, gp_ref[...])


def _mix_out(a_ssd, w_ssd, bp, pall, w_out, g_post, x, *, li, tm):
    m = x.shape[0]
    resident = pl.Buffered(1)
    return pl.pallas_call(
        _mix_out_kernel,
        grid=(m // tm,),
        in_specs=[
            pl.BlockSpec((tm, D_INNER), lambda i: (i, 0)),
            pl.BlockSpec((None, D_INNER, D_MODEL), lambda i: (li, 0, 0), pipeline_mode=resident),
            pl.BlockSpec((tm, D_MODEL), lambda i: (i, 0)),
            pl.BlockSpec((tm, D_MODEL), lambda i: (i, COL_G1 // D_MODEL)),
            pl.BlockSpec((None, D_MODEL, D_MODEL), lambda i: (li, 0, 0), pipeline_mode=resident),
            pl.BlockSpec((1, D_MODEL), lambda i: (0, 0)),
            pl.BlockSpec((tm, D_MODEL), lambda i: (i, 0)),
        ],
        out_specs=pl.BlockSpec((tm, D_MODEL), lambda i: (i, 0)),
        out_shape=jax.ShapeDtypeStruct((m, D_MODEL), F32),
        compiler_params=_params(1),
        name="mix_out",
    )(a_ssd, w_ssd, bp, pall, w_out, g_post, x)


def _mlp_kernel(x_ref, gpre_ref, wu_ref, wd_ref, gpost_ref, o_ref, u_ref, acc_ref, *, nf):
    j = pl.program_id(1)

    @pl.when(j == 0)
    def _():
        u_ref[...] = _rms_scale(x_ref[...], gpre_ref[...]).astype(BF16)
        acc_ref[...] = jnp.zeros_like(acc_ref)

    hid = jnp.maximum(_dot(u_ref[...], wu_ref[...]), 0.0)
    acc_ref[...] += _dot((hid * hid).astype(BF16), wd_ref[...])

    @pl.when(j == nf - 1)
    def _():
        o_ref[...] = x_ref[...] + _rms_scale(acc_ref[...], gpost_ref[...])


def _mlp(x, g_pre, w_up, w_down, g_post, *, li, tm, tf):
    m = x.shape[0]
    nf = D_FF // tf
    return pl.pallas_call(
        functools.partial(_mlp_kernel, nf=nf),
        grid=(m // tm, nf),
        in_specs=[
            pl.BlockSpec((tm, D_MODEL), lambda i, j: (i, 0)),
            pl.BlockSpec((1, D_MODEL), lambda i, j: (0, 0)),
            pl.BlockSpec((None, D_MODEL, tf), lambda i, j: (li, 0, j)),
            pl.BlockSpec((None, tf, D_MODEL), lambda i, j: (li, j, 0)),
            pl.BlockSpec((1, D_MODEL), lambda i, j: (0, 0)),
        ],
        out_specs=pl.BlockSpec((tm, D_MODEL), lambda i, j: (i, 0)),
        out_shape=jax.ShapeDtypeStruct((m, D_MODEL), F32),
        scratch_shapes=[pltpu.VMEM((tm, D_MODEL), BF16), pltpu.VMEM((tm, D_MODEL), F32)],
        compiler_params=_params(2),
        name="mlp",
    )(x, g_pre, w_up, w_down, g_post)


def _tiles(m):
    return min(m, 512)


def _layer_group(x, st_pool, st_conv, st_ssm, pos0, p, big, *, li, ls, b, l):
    m = b * l
    tm = _tiles(m)
    pall, dt = _in_proj(x, p["g_mix_pre"], big["w_in"], p["w_dt"], p["dt_bias"], li=li, tm=tm, tn=2048)

    st16 = jnp.pad(st_pool, ((0, 0), (POOL_HALO - POOL_KEEP, 0), (0, 0)))
    st_hi = st16.astype(BF16)
    st_lo = (st16 - st_hi.astype(F32)).astype(BF16)
    bp = _pool_branch(pall, st_hi, st_lo, big["pool_w"], p["pool_scale"], big["w_pool_proj"],
                      li=li, b=b, l=l, lt=min(l, 256), pos0=pos0)

    stc = jnp.pad(st_conv, ((0, 0), (CONV_HALO - (CONV_W - 1), 0), (0, 0)))
    a_ssd, new_ssm = _ssd_branch(pall, dt, stc, st_ssm, p["conv_w"], p["conv_b"], p["a_log"], p["d_skip"],
                                 p["ssd_norm"], _lane_selector(), ls=ls, b=b, l=l)

    x = _mix_out(a_ssd, big["w_ssd_proj"], bp, pall, big["w_out"], p["g_mix_post"], x, li=li, tm=min(m, 256))
    x = _mlp(x, p["g_mlp_pre"], big["w_up"], big["w_down"], p["g_mlp_post"], li=li, tm=tm, tf=1024)

    p3 = pall.reshape(b, l, P_COLS)
    new_pool = p3[:, l - POOL_KEEP:, COL_POOL:COL_POOL + D_MODEL].astype(F32)
    new_conv = p3[:, l - (CONV_W - 1):, COL_XBC:COL_XBC + D_CONV].astype(F32)
    return x, new_pool, new_conv, new_ssm


def _heads_to_lanes(v):
    lead = v.shape[:-1]
    v = v.reshape(lead + (N_GROUPS, HEADS_PER_GROUP))
    v = jnp.pad(v, [(0, 0)] * len(lead) + [(0, 0), (0, HEAD_LANE_STRIDE - HEADS_PER_GROUP)])
    return v.reshape(lead + (DT_LANES,))


def _lane_selector():
    j = lax.broadcasted_iota(jnp.int32, (SEL_ROWS, SEL_COLS), 0)
    c = lax.broadcasted_iota(jnp.int32, (SEL_ROWS, SEL_COLS), 1)
    blk = c // LANE
    src = jnp.where(blk < HEADS_PER_GROUP, blk, 2 * (blk - HEADS_PER_GROUP) + (c % LANE) // HEAD_DIM)
    return jnp.where(j == SEL_ROW0 + src, 1.0, 0.0).astype(BF16)


def _prep_big(w_in, pool_w, w_pool_proj, w_ssd_proj, w_out, w_up, w_down):
    return {
        "w_in": jnp.swapaxes(w_in, 1, 2).astype(BF16),
        "pool_w": pool_w.astype(BF16),
        "w_pool_proj": w_pool_proj.astype(BF16),
        "w_ssd_proj": w_ssd_proj.astype(BF16),
        "w_out": w_out.astype(BF16),
        "w_up": w_up.astype(BF16),
        "w_down": w_down.astype(BF16),
    }


def _prep_layer(w_in, pool_scale, conv_w, conv_b, dt_bias, a_log, d_skip, ssd_norm, g_mix_pre, g_mix_post,
                g_mlp_pre, g_mlp_post):
    w_dt = _heads_to_lanes(w_in[:, SRC_COL_DT:SRC_COL_GATE]).T.astype(BF16)
    row = lambda v: v.reshape(1, -1)
    return {
        "w_dt": w_dt,
        "dt_bias": _heads_to_lanes(row(dt_bias)),
        "pool_scale": row(pool_scale),
        "conv_w": conv_w,
        "conv_b": row(conv_b),
        "a_log": _heads_to_lanes(row(a_log)),
        "d_skip": row(jnp.repeat(d_skip, HEAD_DIM)),
        "ssd_norm": row(ssd_norm),
        "g_mix_pre": row(g_mix_pre),
        "g_mix_post": row(g_mix_post),
        "g_mlp_pre": row(g_mlp_pre),
        "g_mlp_post": row(g_mlp_post),
    }


@jax.jit
def kernel(x_prompt, x_sample, state_pool, state_conv, state_ssm, w_in, pool_w, pool_scale, conv_w, conv_b,
           dt_bias, a_log, d_skip, ssd_norm, w_pool_proj, w_ssd_proj, w_out, g_mix_pre, g_mix_post,
           w_up, w_down, g_mlp_pre, g_mlp_post):
    bp, lp, _ = x_prompt.shape
    bs, ls, _ = x_sample.shape
    depth = w_in.shape[0]
    past_len = 1024
    zero_pool = jnp.zeros((bp, POOL_KEEP, D_MODEL), F32)
    zero_conv = jnp.zeros((bp, CONV_W - 1, D_CONV), F32)
    zero_ssm = jnp.zeros((1, bp, N_HEADS, HEAD_DIM, D_STATE), F32)
    yp = x_prompt.reshape(bp * lp, D_MODEL)
    ys = x_sample.reshape(bs * ls, D_MODEL)
    big = _prep_big(w_in, pool_w, w_pool_proj, w_ssd_proj, w_out, w_up, w_down)
    outs = [[] for _ in range(6)]
    for li in range(depth):
        p = _prep_layer(w_in[li], pool_scale[li], conv_w[li], conv_b[li], dt_bias[li], a_log[li], d_skip[li],
                        ssd_norm[li], g_mix_pre[li], g_mix_post[li], g_mlp_pre[li], g_mlp_post[li])
        yp, pool_p, conv_p, ssm_p = _layer_group(yp, zero_pool, zero_conv, zero_ssm, 0, p, big,
                                                 li=li, ls=0, b=bp, l=lp)
        ys, pool_s, conv_s, ssm_s = _layer_group(ys, state_pool[li], state_conv[li], state_ssm, past_len, p, big,
                                                 li=li, ls=li, b=bs, l=ls)
        for lst, v in zip(outs, (pool_p, conv_p, ssm_p, pool_s, conv_s, ssm_s)):
            lst.append(v)
    return (yp.reshape(bp, lp, D_MODEL), ys.reshape(bs, ls, D_MODEL)) + tuple(jnp.stack(v) for v in outs)
```

```python
import functools

import jax
import jax.numpy as jnp
from jax import lax
from jax.experimental import pallas as pl
from jax.experimental.pallas import tpu as pltpu

F32 = jnp.float32
BF16 = jnp.bfloat16

D_MODEL = 2048
EPS = 1e-6
POOL_WINDOWS = (2, 4, 8, 16)
POOL_GROUP = D_MODEL // len(POOL_WINDOWS)
POOL_KEEP = max(POOL_WINDOWS) - 1
POOL_HALO = 16
D_INNER = 2 * D_MODEL
HEAD_DIM = 64
N_HEADS = D_INNER // HEAD_DIM
N_GROUPS = 8
HEADS_PER_GROUP = N_HEADS // N_GROUPS
GROUP_W = D_INNER // N_GROUPS
D_STATE = 128
CONV_W = 4
D_CONV = D_INNER + 2 * N_GROUPS * D_STATE
SSD_Q = 128
GROUP_SLOTS = 4
CONV_HALO = 8
D_FF = 4 * D_MODEL
COL_XBC = 0
COL_POOL = COL_XBC + D_CONV
COL_Z = COL_POOL + D_MODEL
COL_G0 = COL_Z + D_INNER
COL_G1 = COL_G0 + D_MODEL
P_COLS = COL_G1 + D_MODEL
SRC_COL_POOL = 0
SRC_COL_XBC = SRC_COL_POOL + D_MODEL + D_INNER
SRC_COL_DT = SRC_COL_XBC + D_CONV
SRC_COL_GATE = SRC_COL_DT + N_HEADS
IN_COLS = SRC_COL_GATE + 2 * D_MODEL
DT_LANES = 128
HEAD_LANE_STRIDE = DT_LANES // N_GROUPS
SEL_ROW0 = DT_LANES - HEAD_LANE_STRIDE
SEL_ROWS = SEL_ROW0 + DT_LANES
SEL_COLS = (HEADS_PER_GROUP + HEADS_PER_GROUP // 2) * 128
NEG_BIG = -1e30
LOG2E = 1.4426950408889634
LANE = 128
VMEM_LIMIT = 56 * 1024 * 1024


def _params(n_axes):
    return pltpu.CompilerParams(dimension_semantics=("arbitrary",) * n_axes, vmem_limit_bytes=VMEM_LIMIT)


def _sigmoid(v):
    return 1.0 / (1.0 + jnp.exp2(v * (-LOG2E)))


def _rms_scale(v, g):
    ms = jnp.mean(v * v, axis=-1, keepdims=True)
    return v * lax.rsqrt(ms + EPS) * g


def _dot(a, b):
    return jnp.dot(a, b, preferred_element_type=F32)


def _dot_nt(a, b):
    return lax.dot_general(a, b, (((1,), (1,)), ((), ())), preferred_element_type=F32)


def _in_proj_kernel(x_ref, g_ref, w_ref, wdt_ref, dtb_ref, p_ref, dt_ref, u_ref, *, tn):
    j = pl.program_id(1)

    @pl.when(j == 0)
    def _():
        u = _rms_scale(x_ref[...], g_ref[...]).astype(BF16)
        u_ref[...] = u
        raw = _dot_nt(u, wdt_ref[...]) + dtb_ref[...]
        dt_ref[...] = jnp.maximum(raw, 0.0) + jnp.log1p(jnp.exp(-jnp.abs(raw)))

    col = j * tn

    @pl.when(col < COL_Z)
    def _():
        p_ref[...] = _dot_nt(u_ref[...], w_ref[0]).astype(BF16)

    @pl.when(col >= COL_Z)
    def _():
        acc = _dot_nt(u_ref[...], w_ref[0])
        sig = _sigmoid(acc)
        p_ref[...] = jnp.where(col >= COL_G0, sig, acc * sig).astype(BF16)


def _in_proj_conv_kernel(x_ref, g_ref, w_ref, wdt_ref, dtb_ref, cw_ref, cb_ref, stc_ref,
                         p_ref, dt_ref, tail_ref, u_ref, halo_ref, *, tn, tm, seq):
    i = pl.program_id(0)
    j = pl.program_id(1)

    @pl.when(j == 0)
    def _():
        u = _rms_scale(x_ref[...], g_ref[...]).astype(BF16)
        u_ref[...] = u
        raw = _dot_nt(u, wdt_ref[...]) + dtb_ref[...]
        dt_ref[...] = jnp.maximum(raw, 0.0) + jnp.log1p(jnp.exp(-jnp.abs(raw)))

    col = j * tn

    @pl.when(col < COL_POOL)
    def _():
        acc = _dot_nt(u_ref[...], w_ref[0])
        starts = (i * tm) % seq == 0
        prev = jnp.where(starts, stc_ref[0], halo_ref[j])
        full = jnp.concatenate([prev, acc], axis=0)
        pre = acc * cw_ref[CONV_W - 1:CONV_W, :] + cb_ref[...]
        for k in range(1, CONV_W):
            tap = CONV_W - 1 - k
            pre = pre + pltpu.roll(full, k, axis=0)[CONV_HALO:] * cw_ref[tap:tap + 1, :]
        p_ref[...] = (pre * _sigmoid(pre)).astype(BF16)
        halo_ref[j] = acc[tm - CONV_HALO:]
        tail_ref[0] = acc[tm - CONV_HALO:]

    @pl.when(jnp.logical_and(col >= COL_POOL, col < COL_Z))
    def _():
        p_ref[...] = _dot_nt(u_ref[...], w_ref[0]).astype(BF16)

    @pl.when(col >= COL_Z)
    def _():
        acc = _dot_nt(u_ref[...], w_ref[0])
        sig = _sigmoid(acc)
        p_ref[...] = jnp.where(col >= COL_G0, sig, acc * sig).astype(BF16)


def _in_proj(x, g, wt, wdt, dtb, *, li, tm, tn, conv=None):
    m = x.shape[0]
    n_x = D_CONV // tn
    n_plain = n_x + (D_MODEL + D_INNER) // tn

    unit = N_HEADS

    def w_rows(i, j):
        plain = jnp.where(j < n_x, (SRC_COL_XBC + j * tn) // unit, (SRC_COL_POOL + (j - n_x) * tn) // unit)
        return (li, jnp.where(j < n_plain, plain, (SRC_COL_GATE + (j - n_plain) * tn) // unit) * unit, 0)

    if conv is not None:
        conv_w, conv_b, stc, seq = conv
        tiles_per_seq = seq // tm

        def xcol(i, j):
            return (0, jnp.minimum(j, n_x - 1))

        def seq_xcol(i, j):
            return (i // tiles_per_seq, 0, jnp.minimum(j, n_x - 1))

        return pl.pallas_call(
            functools.partial(_in_proj_conv_kernel, tn=tn, tm=tm, seq=seq),
            grid=(m // tm, P_COLS // tn),
            in_specs=[
                pl.BlockSpec((tm, D_MODEL), lambda i, j: (i, 0)),
                pl.BlockSpec((1, D_MODEL), lambda i, j: (0, 0)),
                pl.BlockSpec((pl.Element(1), pl.Element(tn), pl.Element(D_MODEL)), w_rows),
                pl.BlockSpec((DT_LANES, D_MODEL), lambda i, j: (0, 0)),
                pl.BlockSpec((1, DT_LANES), lambda i, j: (0, 0)),
                pl.BlockSpec((CONV_W, tn), xcol),
                pl.BlockSpec((1, tn), xcol),
                pl.BlockSpec((1, CONV_HALO, tn), seq_xcol),
            ],
            out_specs=[
                pl.BlockSpec((tm, tn), lambda i, j: (i, j)),
                pl.BlockSpec((tm, DT_LANES), lambda i, j: (i, 0)),
                pl.BlockSpec((1, CONV_HALO, tn), lambda i, j: (i, 0, jnp.minimum(j, n_x - 1))),
            ],
            out_shape=[jax.ShapeDtypeStruct((m, P_COLS), BF16), jax.ShapeDtypeStruct((m, DT_LANES), F32),
                       jax.ShapeDtypeStruct((m // tm, CONV_HALO, D_CONV), F32)],
            scratch_shapes=[pltpu.VMEM((tm, D_MODEL), BF16), pltpu.VMEM((n_x, CONV_HALO, tn), F32)],
            compiler_params=_params(2),
            name="in_proj_conv",
        )(x, g, wt, wdt, dtb, conv_w, conv_b, stc)

    return pl.pallas_call(
        functools.partial(_in_proj_kernel, tn=tn),
        grid=(m // tm, P_COLS // tn),
        in_specs=[
            pl.BlockSpec((tm, D_MODEL), lambda i, j: (i, 0)),
            pl.BlockSpec((1, D_MODEL), lambda i, j: (0, 0)),
            pl.BlockSpec((pl.Element(1), pl.Element(tn), pl.Element(D_MODEL)), w_rows),
            pl.BlockSpec((DT_LANES, D_MODEL), lambda i, j: (0, 0)),
            pl.BlockSpec((1, DT_LANES), lambda i, j: (0, 0)),
        ],
        out_specs=[
            pl.BlockSpec((tm, tn), lambda i, j: (i, j)),
            pl.BlockSpec((tm, DT_LANES), lambda i, j: (i, 0)),
        ],
        out_shape=[jax.ShapeDtypeStruct((m, P_COLS), BF16), jax.ShapeDtypeStruct((m, DT_LANES), F32)],
        scratch_shapes=[pltpu.VMEM((tm, D_MODEL), BF16)],
        compiler_params=_params(2),
        name="in_proj",
    )(x, g, wt, wdt, dtb)


def _pool_kernel(p_ref, g0_ref, sth_ref, stl_ref, pw_ref, ps_ref, wpp_ref, o_ref, hh_ref, hl_ref, a_ref,
                 *, lt, pos0):
    ti = pl.program_id(1)

    @pl.when(ti == 0)
    def _():
        hh_ref[...] = sth_ref[0]
        hl_ref[...] = stl_ref[0]

    cur = p_ref[...]
    diff = lax.broadcasted_iota(jnp.int32, (lt, lt), 0) - lax.broadcasted_iota(jnp.int32, (lt, lt), 1)
    dist = (lax.broadcasted_iota(jnp.int32, (lt, POOL_HALO), 0) + POOL_HALO
            - lax.broadcasted_iota(jnp.int32, (lt, POOL_HALO), 1))
    pos = pos0 + ti * lt + lax.broadcasted_iota(jnp.int32, (lt, 1), 0)
    for g, w in enumerate(POOL_WINDOWS):
        cs = slice(g * POOL_GROUP, (g + 1) * POOL_GROUP)
        band = jnp.where(diff >= 0, jnp.where(diff < w, 1.0, 0.0), 0.0).astype(BF16)
        hband = jnp.where(dist < w, 1.0, 0.0).astype(BF16)
        xg = cur[:, cs]
        win = _dot(band, xg) + _dot(hband, hh_ref[:, cs]) + _dot(hband, hl_ref[:, cs])
        cnt = jnp.minimum(w, pos + 1).astype(F32)
        d = win * (1.0 / cnt) - xg.astype(F32)
        a_ref[:, cs] = (_dot(d.astype(BF16), pw_ref[g]) * ps_ref[:, cs]).astype(BF16)
    o_ref[...] = (_dot(a_ref[...], wpp_ref[...]) * g0_ref[...].astype(F32)).astype(BF16)
    hh_ref[...] = cur[lt - POOL_HALO:, :]
    hl_ref[...] = jnp.zeros_like(hl_ref)


def _pool_branch(pall, st_hi, st_lo, pool_w, pool_scale, wpp, *, li, b, l, lt, pos0):
    nt = l // lt
    m = b * l
    return pl.pallas_call(
        functools.partial(_pool_kernel, lt=lt, pos0=pos0),
        grid=(b, nt),
        in_specs=[
            pl.BlockSpec((lt, D_MODEL), lambda bi, ti: (bi * nt + ti, COL_POOL // D_MODEL)),
            pl.BlockSpec((lt, D_MODEL), lambda bi, ti: (bi * nt + ti, COL_G0 // D_MODEL)),
            pl.BlockSpec((1, POOL_HALO, D_MODEL), lambda bi, ti: (bi, 0, 0)),
            pl.BlockSpec((1, POOL_HALO, D_MODEL), lambda bi, ti: (bi, 0, 0)),
            pl.BlockSpec((None, len(POOL_WINDOWS), POOL_GROUP, POOL_GROUP), lambda bi, ti: (li, 0, 0, 0)),
            pl.BlockSpec((1, D_MODEL), lambda bi, ti: (0, 0)),
            pl.BlockSpec((None, D_MODEL, D_MODEL), lambda bi, ti: (li, 0, 0)),
        ],
        out_specs=pl.BlockSpec((lt, D_MODEL), lambda bi, ti: (bi * nt + ti, 0)),
        out_shape=jax.ShapeDtypeStruct((m, D_MODEL), BF16),
        scratch_shapes=[
            pltpu.VMEM((POOL_HALO, D_MODEL), BF16),
            pltpu.VMEM((POOL_HALO, D_MODEL), BF16),
            pltpu.VMEM((lt, D_MODEL), BF16),
        ],
        compiler_params=_params(2),
        name="pool_branch",
    )(pall, pall, st_hi, st_lo, pool_w, pool_scale, wpp)


def _split3(v):
    hi = v.astype(BF16)
    r1 = v - hi.astype(F32)
    mid = r1.astype(BF16)
    lo = (r1 - mid.astype(F32)).astype(BF16)
    return hi, mid, lo


def _ssd_kernel(xbc_ref, zs_ref, dt_ref, stc_ref, sts_ref, cw_ref, cb_ref, alog_ref, dsk_ref, nw_ref, sel_ref,
                o_ref, ns_ref, xc_ref, xa_ref, ht_ref, dtt_ref, cumt_ref, colb2_ref, yst2_ref, y2_ref,
                *, lv, nt, preconv):
    q = SSD_Q
    ti = pl.program_id(1)

    @pl.when(ti == 0)
    def _():
        if not preconv:
            xc_ref[0:CONV_HALO, :] = stc_ref[0]
        for g in range(N_GROUPS):
            sg = sts_ref[0, g * HEADS_PER_GROUP:(g + 1) * HEADS_PER_GROUP]
            ht_ref[g] = sg.reshape(GROUP_W, D_STATE).T

    if preconv:
        act_ref = xbc_ref
    else:
        act_ref = xa_ref
        if lv < q:
            xa_ref[lv:q, :] = jnp.zeros((q - lv, D_CONV), F32)
        cwid = 256

        def conv_body(c, carry):
            cs = pl.ds(pl.multiple_of(c * cwid, cwid), cwid)
            xc_ref[CONV_HALO:CONV_HALO + lv, cs] = xbc_ref[:, cs].astype(F32)
            full = xc_ref[0:CONV_HALO + lv, cs]
            acc = full[CONV_HALO:] * cw_ref[CONV_W - 1:CONV_W, cs] + cb_ref[:, cs]
            for k in range(1, CONV_W):
                tap = CONV_W - 1 - k
                acc = acc + pltpu.roll(full, k, axis=0)[CONV_HALO:] * cw_ref[tap:tap + 1, cs]
            xa_ref[0:lv, cs] = acc * _sigmoid(acc)
            return carry

        lax.fori_loop(0, D_CONV // cwid, conv_body, 0)
        xc_ref[0:CONV_HALO, :] = xc_ref[lv:lv + CONV_HALO, :]

    if lv < q:
        dt_tm = jnp.concatenate([dt_ref[...], jnp.zeros((q - lv, DT_LANES), F32)], axis=0)
    else:
        dt_tm = dt_ref[...]
    a2 = dt_tm * (-LOG2E * jnp.exp(alog_ref[...]))
    row_i = lax.broadcasted_iota(jnp.int32, (q, q), 0)
    col_i = lax.broadcasted_iota(jnp.int32, (q, q), 1)
    causal = row_i >= col_i
    lower = jnp.where(causal, 1.0, 0.0).astype(BF16)
    hi, mid, lo = _split3(a2)
    cum = _dot(lower, hi) + _dot(lower, mid) + _dot(lower, lo)
    c2 = jnp.concatenate(_split3(cum)[0:2], axis=1)
    cumt_ref[...] = cum.T
    dtt_ref[...] = dt_tm.T
    lane_row = lax.broadcasted_iota(jnp.int32, (1, LANE), 1)
    half0 = jnp.where(lane_row < HEAD_DIM, 1.0, 0.0).astype(BF16)
    half1 = jnp.where(lane_row >= HEAD_DIM, 1.0, 0.0).astype(BF16)

    def one_group(g, slot):
        colb_ref, yst_ref, y_ref = colb2_ref.at[slot], yst2_ref.at[slot], y2_ref.at[slot]
        go = g * GROUP_W
        bg = act_ref[:, pl.ds(pl.multiple_of(D_INNER + g * D_STATE, D_STATE), D_STATE)]
        cg = act_ref[:, pl.ds(pl.multiple_of(D_INNER + N_GROUPS * D_STATE + g * D_STATE, D_STATE), D_STATE)]
        cg16 = cg.astype(BF16)
        cb = lax.dot_general(cg16, bg.astype(BF16), (((1,), (1,)), ((), ())), preferred_element_type=F32)
        bt = bg.astype(F32).T
        selg = sel_ref[pl.ds(pl.multiple_of(SEL_ROW0 - HEAD_LANE_STRIDE * g, HEAD_LANE_STRIDE), DT_LANES), :]
        colb_ref[...] = _dot(c2, jnp.concatenate([selg, selg], axis=0))
        yst_ref[...] = _dot(cg16, ht_ref[g].astype(BF16))
        for j in range(HEADS_PER_GROUP // 2):
            pair = slice(j * LANE, (j + 1) * LANE)
            mixes = []
            bws = []
            lasts = []
            for e in range(2):
                r = 2 * j + e
                h = g * HEAD_LANE_STRIDE + r
                cum_row = cumt_ref[pl.ds(h, 1), :]
                last = cum_row[:, q - 1:q]
                src_row = cum_row - jnp.log2(dtt_ref[pl.ds(h, 1), :])
                cum_col = colb_ref[:, r * LANE:(r + 1) * LANE]
                mixes.append((cb * jnp.exp2(jnp.where(causal, cum_col - src_row, NEG_BIG))).astype(BF16))
                bws.append((bt * jnp.exp2(last - src_row)).astype(BF16))
                lasts.append(jnp.exp2(last))
            xs = act_ref[:, pl.ds(pl.multiple_of(go + j * LANE, LANE), LANE)]
            xs16 = xs.astype(BF16)
            rhs = jnp.concatenate([xs16 * half0, xs16 * half1], axis=0)
            dsk = dsk_ref[:, pl.ds(pl.multiple_of(go + j * LANE, LANE), LANE)]
            cum_pair = colb_ref[:, (HEADS_PER_GROUP + j) * LANE:(HEADS_PER_GROUP + j + 1) * LANE]
            y_ref[:, pair] = (_dot(jnp.concatenate(mixes, axis=1), rhs)
                              + yst_ref[:, pair] * jnp.exp2(cum_pair) + dsk * xs.astype(F32))
            keep = jnp.where(lane_row < HEAD_DIM, lasts[0], lasts[1])
            ht_ref[g, :, pair] = ht_ref[g, :, pair] * keep + _dot(jnp.concatenate(bws, axis=1), rhs)
        gs = pl.ds(pl.multiple_of(go, GROUP_W), GROUP_W)
        gated = y_ref[0:lv, :] * zs_ref[:, gs].astype(F32)
        o_ref[:, gs] = _rms_scale(gated, nw_ref[:, gs]).astype(BF16)

    def group_batch_body(i, carry):
        for slot in range(GROUP_SLOTS):
            one_group(GROUP_SLOTS * i + slot, slot)
        return carry

    lax.fori_loop(0, N_GROUPS // GROUP_SLOTS, group_batch_body, 0)

    @pl.when(ti == nt - 1)
    def _():
        for g in range(N_GROUPS):
            ns_ref[0, g * HEADS_PER_GROUP:(g + 1) * HEADS_PER_GROUP] = (
                ht_ref[g].T.reshape(HEADS_PER_GROUP, HEAD_DIM, D_STATE))


def _ssd_branch(pall, dt, st_conv, st_ssm, conv_w, conv_b, alog_row, dskip_row, norm_w, sel, *, ls, b, l,
                preconv):
    lv = min(l, SSD_Q)
    nt = l // lv
    m = b * l
    return pl.pallas_call(
        functools.partial(_ssd_kernel, lv=lv, nt=nt, preconv=preconv),
        grid=(b, nt),
        in_specs=[
            pl.BlockSpec((lv, D_CONV), lambda bi, ti: (bi * nt + ti, COL_XBC // D_CONV)),
            pl.BlockSpec((lv, D_INNER), lambda bi, ti: (bi * nt + ti, COL_Z // D_INNER)),
            pl.BlockSpec((lv, DT_LANES), lambda bi, ti: (bi * nt + ti, 0)),
            pl.BlockSpec((1, CONV_HALO, D_CONV), lambda bi, ti: (bi, 0, 0)),
            pl.BlockSpec((None, 1, N_HEADS, HEAD_DIM, D_STATE), lambda bi, ti: (ls, bi, 0, 0, 0)),
            pl.BlockSpec((CONV_W, D_CONV), lambda bi, ti: (0, 0)),
            pl.BlockSpec((1, D_CONV), lambda bi, ti: (0, 0)),
            pl.BlockSpec((1, DT_LANES), lambda bi, ti: (0, 0)),
            pl.BlockSpec((1, D_INNER), lambda bi, ti: (0, 0)),
            pl.BlockSpec((1, D_INNER), lambda bi, ti: (0, 0)),
            pl.BlockSpec((SEL_ROWS, SEL_COLS), lambda bi, ti: (0, 0)),
        ],
        out_specs=[
            pl.BlockSpec((lv, D_INNER), lambda bi, ti: (bi * nt + ti, 0)),
            pl.BlockSpec((1, N_HEADS, HEAD_DIM, D_STATE), lambda bi, ti: (bi, 0, 0, 0)),
        ],
        out_shape=[
            jax.ShapeDtypeStruct((m, D_INNER), BF16),
            jax.ShapeDtypeStruct((b, N_HEADS, HEAD_DIM, D_STATE), F32),
        ],
        scratch_shapes=[
            pltpu.VMEM((CONV_HALO + SSD_Q, D_CONV), F32),
            pltpu.VMEM((SSD_Q, D_CONV), F32),
            pltpu.VMEM((N_GROUPS, D_STATE, GROUP_W), F32),
            pltpu.VMEM((DT_LANES, SSD_Q), F32),
            pltpu.VMEM((DT_LANES, SSD_Q), F32),
            pltpu.VMEM((GROUP_SLOTS, SSD_Q, SEL_COLS), F32),
            pltpu.VMEM((GROUP_SLOTS, SSD_Q, GROUP_W), F32),
            pltpu.VMEM((GROUP_SLOTS, SSD_Q, GROUP_W), F32),
        ],
        compiler_params=_params(2),
        name="ssd_branch",
    )(pall, pall, dt, st_conv, st_ssm, conv_w, conv_b, alog_row, dskip_row, norm_w, sel)


def _mix_out_kernel(a_ref, ws_ref, bp_ref, g1_ref, wo_ref, gp_ref, x_ref, o_ref):
    br_ssd = _dot(a_ref[...], ws_ref[...])
    merged = (bp_ref[...].astype(F32) + g1_ref[...].astype(F32) * br_ssd).astype(BF16)
    out = _dot(merged, wo_ref[...])
    o_ref[...] = x_ref[...] + _rms_scale(out, gp_ref[...])


def _mix_out(a_ssd, w_ssd, bp, pall, w_out, g_post, x, *, li, tm):
    m = x.shape[0]
    resident = pl.Buffered(1)
    return pl.pallas_call(
        _mix_out_kernel,
        grid=(m // tm,),
        in_specs=[
            pl.BlockSpec((tm, D_INNER), lambda i: (i, 0)),
            pl.BlockSpec((None, D_INNER, D_MODEL), lambda i: (li, 0, 0), pipeline_mode=resident),
            pl.BlockSpec((tm, D_MODEL), lambda i: (i, 0)),
            pl.BlockSpec((tm, D_MODEL), lambda i: (i, COL_G1 // D_MODEL)),
            pl.BlockSpec((None, D_MODEL, D_MODEL), lambda i: (li, 0, 0), pipeline_mode=resident),
            pl.BlockSpec((1, D_MODEL), lambda i: (0, 0)),
            pl.BlockSpec((tm, D_MODEL), lambda i: (i, 0)),
        ],
        out_specs=pl.BlockSpec((tm, D_MODEL), lambda i: (i, 0)),
        out_shape=jax.ShapeDtypeStruct((m, D_MODEL), F32),
        compiler_params=_params(1),
        name="mix_out",
    )(a_ssd, w_ssd, bp, pall, w_out, g_post, x)


def _mlp_kernel(x_ref, gpre_ref, wu_ref, wd_ref, gpost_ref, o_ref, u_ref, acc_ref, *, nf):
    j = pl.program_id(1)

    @pl.when(j == 0)
    def _():
        u_ref[...] = _rms_scale(x_ref[...], gpre_ref[...]).astype(BF16)
        acc_ref[...] = jnp.zeros_like(acc_ref)

    hid = jnp.maximum(_dot(u_ref[...], wu_ref[...]), 0.0)
    acc_ref[...] += _dot((hid * hid).astype(BF16), wd_ref[...])

    @pl.when(j == nf - 1)
    def _():
        o_ref[...] = x_ref[...] + _rms_scale(acc_ref[...], gpost_ref[...])


def _mlp(x, g_pre, w_up, w_down, g_post, *, li, tm, tf):
    m = x.shape[0]
    nf = D_FF // tf
    return pl.pallas_call(
        functools.partial(_mlp_kernel, nf=nf),
        grid=(m // tm, nf),
        in_specs=[
            pl.BlockSpec((tm, D_MODEL), lambda i, j: (i, 0)),
            pl.BlockSpec((1, D_MODEL), lambda i, j: (0, 0)),
            pl.BlockSpec((None, D_MODEL, tf), lambda i, j: (li, 0, j)),
            pl.BlockSpec((None, tf, D_MODEL), lambda i, j: (li, j, 0)),
            pl.BlockSpec((1, D_MODEL), lambda i, j: (0, 0)),
        ],
        out_specs=pl.BlockSpec((tm, D_MODEL), lambda i, j: (i, 0)),
        out_shape=jax.ShapeDtypeStruct((m, D_MODEL), F32),
        scratch_shapes=[pltpu.VMEM((tm, D_MODEL), BF16), pltpu.VMEM((tm, D_MODEL), F32)],
        compiler_params=_params(2),
        name="mlp",
    )(x, g_pre, w_up, w_down, g_post)


def _tiles(m):
    return min(m, 512)


def _layer_group(x, st_pool, st_conv, st_ssm, pos0, p, big, *, li, ls, b, l):
    m = b * l
    tm = _tiles(m)
    stc = jnp.pad(st_conv, ((0, 0), (CONV_HALO - (CONV_W - 1), 0), (0, 0)))
    preconv = l % tm == 0
    if preconv:
        pall, dt, tail = _in_proj(x, p["g_mix_pre"], big["w_in"], p["w_dt"], p["dt_bias"], li=li, tm=tm, tn=2048,
                                  conv=(p["conv_w"], p["conv_b"], stc, l))
    else:
        pall, dt = _in_proj(x, p["g_mix_pre"], big["w_in"], p["w_dt"], p["dt_bias"], li=li, tm=tm, tn=2048)

    st16 = jnp.pad(st_pool, ((0, 0), (POOL_HALO - POOL_KEEP, 0), (0, 0)))
    st_hi = st16.astype(BF16)
    st_lo = (st16 - st_hi.astype(F32)).astype(BF16)
    bp = _pool_branch(pall, st_hi, st_lo, big["pool_w"], p["pool_scale"], big["w_pool_proj"],
                      li=li, b=b, l=l, lt=min(l, 256), pos0=pos0)

    a_ssd, new_ssm = _ssd_branch(pall, dt, stc, st_ssm, p["conv_w"], p["conv_b"], p["a_log"], p["d_skip"],
                                 p["ssd_norm"], _lane_selector(), ls=ls, b=b, l=l, preconv=preconv)

    x = _mix_out(a_ssd, big["w_ssd_proj"], bp, pall, big["w_out"], p["g_mix_post"], x, li=li, tm=min(m, 256))
    x = _mlp(x, p["g_mlp_pre"], big["w_up"], big["w_down"], p["g_mlp_post"], li=li, tm=tm, tf=1024)

    p3 = pall.reshape(b, l, P_COLS)
    new_pool = p3[:, l - POOL_KEEP:, COL_POOL:COL_POOL + D_MODEL].astype(F32)
    if preconv:
        new_conv = tail.reshape(b, l // tm, CONV_HALO, D_CONV)[:, -1, CONV_HALO - (CONV_W - 1):, :]
    else:
        new_conv = p3[:, l - (CONV_W - 1):, COL_XBC:COL_XBC + D_CONV].astype(F32)
    return x, new_pool, new_conv, new_ssm


def _heads_to_lanes(v):
    lead = v.shape[:-1]
    v = v.reshape(lead + (N_GROUPS, HEADS_PER_GROUP))
    v = jnp.pad(v, [(0, 0)] * len(lead) + [(0, 0), (0, HEAD_LANE_STRIDE - HEADS_PER_GROUP)])
    return v.reshape(lead + (DT_LANES,))


def _lane_selector():
    j = lax.broadcasted_iota(jnp.int32, (SEL_ROWS, SEL_COLS), 0)
    c = lax.broadcasted_iota(jnp.int32, (SEL_ROWS, SEL_COLS), 1)
    blk = c // LANE
    src = jnp.where(blk < HEADS_PER_GROUP, blk, 2 * (blk - HEADS_PER_GROUP) + (c % LANE) // HEAD_DIM)
    return jnp.where(j == SEL_ROW0 + src, 1.0, 0.0).astype(BF16)


def _prep_big(w_in, pool_w, w_pool_proj, w_ssd_proj, w_out, w_up, w_down):
    return {
        "w_in": jnp.swapaxes(w_in, 1, 2).astype(BF16),
        "pool_w": pool_w.astype(BF16),
        "w_pool_proj": w_pool_proj.astype(BF16),
        "w_ssd_proj": w_ssd_proj.astype(BF16),
        "w_out": w_out.astype(BF16),
        "w_up": w_up.astype(BF16),
        "w_down": w_down.astype(BF16),
    }


def _prep_layer(wt, pool_scale, conv_w, conv_b, dt_bias, a_log, d_skip, ssd_norm, g_mix_pre, g_mix_post,
                g_mlp_pre, g_mlp_post):
    w_dt = wt[SRC_COL_DT:SRC_COL_GATE].reshape(N_GROUPS, HEADS_PER_GROUP, D_MODEL)
    w_dt = jnp.pad(w_dt, ((0, 0), (0, HEAD_LANE_STRIDE - HEADS_PER_GROUP), (0, 0))).reshape(DT_LANES, D_MODEL)
    row = lambda v: v.reshape(1, -1)
    return {
        "w_dt": w_dt,
        "dt_bias": _heads_to_lanes(row(dt_bias)),
        "pool_scale": row(pool_scale),
        "conv_w": conv_w,
        "conv_b": row(conv_b),
        "a_log": _heads_to_lanes(row(a_log)),
        "d_skip": row(jnp.repeat(d_skip, HEAD_DIM)),
        "ssd_norm": row(ssd_norm),
        "g_mix_pre": row(g_mix_pre),
        "g_mix_post": row(g_mix_post),
        "g_mlp_pre": row(g_mlp_pre),
        "g_mlp_post": row(g_mlp_post),
    }


@jax.jit
def kernel(x_prompt, x_sample, state_pool, state_conv, state_ssm, w_in, pool_w, pool_scale, conv_w, conv_b,
           dt_bias, a_log, d_skip, ssd_norm, w_pool_proj, w_ssd_proj, w_out, g_mix_pre, g_mix_post,
           w_up, w_down, g_mlp_pre, g_mlp_post):
    bp, lp, _ = x_prompt.shape
    bs, ls, _ = x_sample.shape
    depth = w_in.shape[0]
    past_len = 1024
    zero_pool = jnp.zeros((bp, POOL_KEEP, D_MODEL), F32)
    zero_conv = jnp.zeros((bp, CONV_W - 1, D_CONV), F32)
    zero_ssm = jnp.zeros((1, bp, N_HEADS, HEAD_DIM, D_STATE), F32)
    yp = x_prompt.reshape(bp * lp, D_MODEL)
    ys = x_sample.reshape(bs * ls, D_MODEL)
    big = _prep_big(w_in, pool_w, w_pool_proj, w_ssd_proj, w_out, w_up, w_down)
    outs = [[] for _ in range(6)]
    for li in range(depth):
        p = _prep_layer(big["w_in"][li], pool_scale[li], conv_w[li], conv_b[li], dt_bias[li], a_log[li], d_skip[li],
                        ssd_norm[li], g_mix_pre[li], g_mix_post[li], g_mlp_pre[li], g_mlp_post[li])
        yp, pool_p, conv_p, ssm_p = _layer_group(yp, zero_pool, zero_conv, zero_ssm, 0, p, big,
                                                 li=li, ls=0, b=bp, l=lp)
        ys, pool_s, conv_s, ssm_s = _layer_group(ys, state_pool[li], state_conv[li], state_ssm, past_len, p, big,
                                                 li=li, ls=li, b=bs, l=ls)
        for lst, v in zip(outs, (pool_p, conv_p, ssm_p, pool_s, conv_s, ssm_s)):
            lst.append(v)
    return (yp.reshape(bp, lp, D_MODEL), ys.reshape(bs, ls, D_MODEL)) + tuple(jnp.stack(v) for v in outs)
```

```python
import functools

import jax
import jax.numpy as jnp
from jax import lax
from jax.experimental import pallas as pl
from jax.experimental.pallas import tpu as pltpu

F32 = jnp.float32
BF16 = jnp.bfloat16

D_MODEL = 2048
EPS = 1e-6
POOL_WINDOWS = (2, 4, 8, 16)
POOL_GROUP = D_MODEL // len(POOL_WINDOWS)
POOL_KEEP = max(POOL_WINDOWS) - 1
POOL_HALO = 16
D_INNER = 2 * D_MODEL
HEAD_DIM = 64
N_HEADS = D_INNER // HEAD_DIM
N_GROUPS = 8
HEADS_PER_GROUP = N_HEADS // N_GROUPS
GROUP_W = D_INNER // N_GROUPS
D_STATE = 128
CONV_W = 4
D_CONV = D_INNER + 2 * N_GROUPS * D_STATE
SSD_Q = 128
GROUP_SLOTS = 4
CONV_HALO = 8
D_FF = 4 * D_MODEL
COL_XBC = 0
COL_POOL = COL_XBC + D_CONV
COL_Z = COL_POOL + D_MODEL
COL_G0 = COL_Z + D_INNER
COL_G1 = COL_G0 + D_MODEL
P_COLS = COL_G1 + D_MODEL
SRC_COL_POOL = 0
SRC_COL_XBC = SRC_COL_POOL + D_MODEL + D_INNER
SRC_COL_DT = SRC_COL_XBC + D_CONV
SRC_COL_GATE = SRC_COL_DT + N_HEADS
IN_COLS = SRC_COL_GATE + 2 * D_MODEL
DT_LANES = 128
HEAD_LANE_STRIDE = DT_LANES // N_GROUPS
SEL_ROW0 = DT_LANES - HEAD_LANE_STRIDE
SEL_ROWS = SEL_ROW0 + DT_LANES
SEL_COLS = (HEADS_PER_GROUP + HEADS_PER_GROUP // 2) * 128
NEG_BIG = -1e30
LOG2E = 1.4426950408889634
LANE = 128
VMEM_LIMIT = 56 * 1024 * 1024


def _params(n_axes):
    return pltpu.CompilerParams(dimension_semantics=("arbitrary",) * n_axes, vmem_limit_bytes=VMEM_LIMIT)


def _sigmoid(v):
    return 1.0 / (1.0 + jnp.exp2(v * (-LOG2E)))


def _rms_scale(v, g):
    ms = jnp.mean(v * v, axis=-1, keepdims=True)
    return v * lax.rsqrt(ms + EPS) * g


def _dot(a, b):
    return jnp.dot(a, b, preferred_element_type=F32)


def _dot_nt(a, b):
    return lax.dot_general(a, b, (((1,), (1,)), ((), ())), preferred_element_type=F32)


def _in_proj_kernel(x_ref, g_ref, w_ref, wdt_ref, dtb_ref, p_ref, dt_ref, u_ref, *, tn):
    j = pl.program_id(1)

    @pl.when(j == 0)
    def _():
        u = _rms_scale(x_ref[...], g_ref[...]).astype(BF16)
        u_ref[...] = u
        raw = _dot_nt(u, wdt_ref[...]) + dtb_ref[...]
        dt_ref[...] = jnp.maximum(raw, 0.0) + jnp.log1p(jnp.exp(-jnp.abs(raw)))

    col = j * tn

    @pl.when(col < COL_Z)
    def _():
        p_ref[...] = _dot_nt(u_ref[...], w_ref[0]).astype(BF16)

    @pl.when(col >= COL_Z)
    def _():
        acc = _dot_nt(u_ref[...], w_ref[0])
        sig = _sigmoid(acc)
        p_ref[...] = jnp.where(col >= COL_G0, sig, acc * sig).astype(BF16)


def _in_proj(x, g, wt, wdt, dtb, *, li, tm, tn):
    m = x.shape[0]
    n_x = D_CONV // tn
    n_plain = n_x + (D_MODEL + D_INNER) // tn

    unit = N_HEADS

    def w_rows(i, j):
        plain = jnp.where(j < n_x, (SRC_COL_XBC + j * tn) // unit, (SRC_COL_POOL + (j - n_x) * tn) // unit)
        return (li, jnp.where(j < n_plain, plain, (SRC_COL_GATE + (j - n_plain) * tn) // unit) * unit, 0)

    return pl.pallas_call(
        functools.partial(_in_proj_kernel, tn=tn),
        grid=(m // tm, P_COLS // tn),
        in_specs=[
            pl.BlockSpec((tm, D_MODEL), lambda i, j: (i, 0)),
            pl.BlockSpec((1, D_MODEL), lambda i, j: (0, 0)),
            pl.BlockSpec((pl.Element(1), pl.Element(tn), pl.Element(D_MODEL)), w_rows),
            pl.BlockSpec((DT_LANES, D_MODEL), lambda i, j: (0, 0)),
            pl.BlockSpec((1, DT_LANES), lambda i, j: (0, 0)),
        ],
        out_specs=[
            pl.BlockSpec((tm, tn), lambda i, j: (i, j)),
            pl.BlockSpec((tm, DT_LANES), lambda i, j: (i, 0)),
        ],
        out_shape=[jax.ShapeDtypeStruct((m, P_COLS), BF16), jax.ShapeDtypeStruct((m, DT_LANES), F32)],
        scratch_shapes=[pltpu.VMEM((tm, D_MODEL), BF16)],
        compiler_params=_params(2),
        name="in_proj",
    )(x, g, wt, wdt, dtb)


def _pool_kernel(p_ref, g0_ref, sth_ref, stl_ref, pw_ref, ps_ref, wpp_ref, o_ref, hh_ref, hl_ref, a_ref,
                 *, lt, pos0):
    ti = pl.program_id(1)

    @pl.when(ti == 0)
    def _():
        hh_ref[...] = sth_ref[0]
        hl_ref[...] = stl_ref[0]

    cur = p_ref[...]
    diff = lax.broadcasted_iota(jnp.int32, (lt, lt), 0) - lax.broadcasted_iota(jnp.int32, (lt, lt), 1)
    dist = (lax.broadcasted_iota(jnp.int32, (lt, POOL_HALO), 0) + POOL_HALO
            - lax.broadcasted_iota(jnp.int32, (lt, POOL_HALO), 1))
    pos = pos0 + ti * lt + lax.broadcasted_iota(jnp.int32, (lt, 1), 0)
    for g, w in enumerate(POOL_WINDOWS):
        cs = slice(g * POOL_GROUP, (g + 1) * POOL_GROUP)
        band = jnp.where(diff >= 0, jnp.where(diff < w, 1.0, 0.0), 0.0).astype(BF16)
        hband = jnp.where(dist < w, 1.0, 0.0).astype(BF16)
        xg = cur[:, cs]
        halo = jnp.concatenate([hh_ref[:, cs], hl_ref[:, cs]], axis=0)
        win = _dot(band, xg) + _dot(jnp.concatenate([hband, hband], axis=1), halo)
        cnt = jnp.minimum(w, pos + 1).astype(F32)
        d = win * (1.0 / cnt) - xg.astype(F32)
        a_ref[:, cs] = (_dot(d.astype(BF16), pw_ref[g]) * ps_ref[:, cs]).astype(BF16)
    o_ref[...] = (_dot(a_ref[...], wpp_ref[...]) * g0_ref[...].astype(F32)).astype(BF16)
    hh_ref[...] = cur[lt - POOL_HALO:, :]
    hl_ref[...] = jnp.zeros_like(hl_ref)


def _pool_branch(pall, st_hi, st_lo, pool_w, pool_scale, wpp, *, li, b, l, lt, pos0):
    nt = l // lt
    m = b * l
    return pl.pallas_call(
        functools.partial(_pool_kernel, lt=lt, pos0=pos0),
        grid=(b, nt),
        in_specs=[
            pl.BlockSpec((lt, D_MODEL), lambda bi, ti: (bi * nt + ti, COL_POOL // D_MODEL)),
            pl.BlockSpec((lt, D_MODEL), lambda bi, ti: (bi * nt + ti, COL_G0 // D_MODEL)),
            pl.BlockSpec((1, POOL_HALO, D_MODEL), lambda bi, ti: (bi, 0, 0)),
            pl.BlockSpec((1, POOL_HALO, D_MODEL), lambda bi, ti: (bi, 0, 0)),
            pl.BlockSpec((None, len(POOL_WINDOWS), POOL_GROUP, POOL_GROUP), lambda bi, ti: (li, 0, 0, 0)),
            pl.BlockSpec((1, D_MODEL), lambda bi, ti: (0, 0)),
            pl.BlockSpec((None, D_MODEL, D_MODEL), lambda bi, ti: (li, 0, 0)),
        ],
        out_specs=pl.BlockSpec((lt, D_MODEL), lambda bi, ti: (bi * nt + ti, 0)),
        out_shape=jax.ShapeDtypeStruct((m, D_MODEL), BF16),
        scratch_shapes=[
            pltpu.VMEM((POOL_HALO, D_MODEL), BF16),
            pltpu.VMEM((POOL_HALO, D_MODEL), BF16),
            pltpu.VMEM((lt, D_MODEL), BF16),
        ],
        compiler_params=_params(2),
        name="pool_branch",
    )(pall, pall, st_hi, st_lo, pool_w, pool_scale, wpp)


def _split3(v):
    hi = v.astype(BF16)
    r1 = v - hi.astype(F32)
    mid = r1.astype(BF16)
    lo = (r1 - mid.astype(F32)).astype(BF16)
    return hi, mid, lo


def _ssd_kernel(xbc_ref, zs_ref, dt_ref, stc_ref, sts_ref, cw_ref, cb_ref, alog_ref, dsk_ref, nw_ref, sel_ref,
                o_ref, ns_ref, xc_ref, xa_ref, ht_ref, dtt_ref, cumt_ref, colb2_ref, yst2_ref, y2_ref, *, lv, nt):
    q = SSD_Q
    ti = pl.program_id(1)

    @pl.when(ti == 0)
    def _():
        xc_ref[0:CONV_HALO, :] = stc_ref[0]
        for g in range(N_GROUPS):
            sg = sts_ref[0, g * HEADS_PER_GROUP:(g + 1) * HEADS_PER_GROUP]
            ht_ref[g] = sg.reshape(GROUP_W, D_STATE).T

    if lv < q:
        xa_ref[lv:q, :] = jnp.zeros((q - lv, D_CONV), F32)
    cwid = 256

    def conv_body(c, carry):
        cs = pl.ds(pl.multiple_of(c * cwid, cwid), cwid)
        xc_ref[CONV_HALO:CONV_HALO + lv, cs] = xbc_ref[:, cs].astype(F32)
        full = xc_ref[0:CONV_HALO + lv, cs]
        acc = full[CONV_HALO:] * cw_ref[CONV_W - 1:CONV_W, cs] + cb_ref[:, cs]
        for k in range(1, CONV_W):
            tap = CONV_W - 1 - k
            acc = acc + pltpu.roll(full, k, axis=0)[CONV_HALO:] * cw_ref[tap:tap + 1, cs]
        xa_ref[0:lv, cs] = acc * _sigmoid(acc)
        return carry

    lax.fori_loop(0, D_CONV // cwid, conv_body, 0)
    xc_ref[0:CONV_HALO, :] = xc_ref[lv:lv + CONV_HALO, :]

    if lv < q:
        dt_tm = jnp.concatenate([dt_ref[...], jnp.zeros((q - lv, DT_LANES), F32)], axis=0)
    else:
        dt_tm = dt_ref[...]
    a2 = dt_tm * (-LOG2E * jnp.exp(alog_ref[...]))
    row_i = lax.broadcasted_iota(jnp.int32, (q, q), 0)
    col_i = lax.broadcasted_iota(jnp.int32, (q, q), 1)
    causal = row_i >= col_i
    lower = jnp.where(causal, 1.0, 0.0).astype(BF16)
    hi, mid, lo = _split3(a2)
    cum = _dot(lower, hi) + _dot(lower, mid) + _dot(lower, lo)
    c2 = jnp.concatenate(_split3(cum)[0:2], axis=1)
    cumt_ref[...] = cum.T
    dtt_ref[...] = dt_tm.T
    lane_row = lax.broadcasted_iota(jnp.int32, (1, LANE), 1)
    half0 = jnp.where(lane_row < HEAD_DIM, 1.0, 0.0).astype(BF16)
    half1 = jnp.where(lane_row >= HEAD_DIM, 1.0, 0.0).astype(BF16)

    def one_group(g, slot):
        colb_ref, yst_ref, y_ref = colb2_ref.at[slot], yst2_ref.at[slot], y2_ref.at[slot]
        go = g * GROUP_W
        bg = xa_ref[:, pl.ds(pl.multiple_of(D_INNER + g * D_STATE, D_STATE), D_STATE)]
        cg = xa_ref[:, pl.ds(pl.multiple_of(D_INNER + N_GROUPS * D_STATE + g * D_STATE, D_STATE), D_STATE)]
        cg16 = cg.astype(BF16)
        cb = lax.dot_general(cg16, bg.astype(BF16), (((1,), (1,)), ((), ())), preferred_element_type=F32)
        bt = bg.T
        selg = sel_ref[pl.ds(pl.multiple_of(SEL_ROW0 - HEAD_LANE_STRIDE * g, HEAD_LANE_STRIDE), DT_LANES), :]
        colb_ref[...] = _dot(c2, jnp.concatenate([selg, selg], axis=0))
        yst_ref[...] = _dot(cg16, ht_ref[g].astype(BF16))
        for j in range(HEADS_PER_GROUP // 2):
            pair = slice(j * LANE, (j + 1) * LANE)
            mixes = []
            bws = []
            lasts = []
            for e in range(2):
                r = 2 * j + e
                h = g * HEAD_LANE_STRIDE + r
                cum_row = cumt_ref[pl.ds(h, 1), :]
                last = cum_row[:, q - 1:q]
                src_row = cum_row - jnp.log2(dtt_ref[pl.ds(h, 1), :])
                cum_col = colb_ref[:, r * LANE:(r + 1) * LANE]
                mixes.append((cb * jnp.exp2(jnp.where(causal, cum_col - src_row, NEG_BIG))).astype(BF16))
                bws.append((bt * jnp.exp2(last - src_row)).astype(BF16))
                lasts.append(jnp.exp2(last))
            xs = xa_ref[:, pl.ds(pl.multiple_of(go + j * LANE, LANE), LANE)]
            xs16 = xs.astype(BF16)
            rhs = jnp.concatenate([xs16 * half0, xs16 * half1], axis=0)
            dsk = dsk_ref[:, pl.ds(pl.multiple_of(go + j * LANE, LANE), LANE)]
            cum_pair = colb_ref[:, (HEADS_PER_GROUP + j) * LANE:(HEADS_PER_GROUP + j + 1) * LANE]
            y_ref[:, pair] = (_dot(jnp.concatenate(mixes, axis=1), rhs)
                              + yst_ref[:, pair] * jnp.exp2(cum_pair) + dsk * xs)
            keep = jnp.where(lane_row < HEAD_DIM, lasts[0], lasts[1])
            ht_ref[g, :, pair] = ht_ref[g, :, pair] * keep + _dot(jnp.concatenate(bws, axis=1), rhs)
        gs = pl.ds(pl.multiple_of(go, GROUP_W), GROUP_W)
        gated = y_ref[0:lv, :] * zs_ref[:, gs].astype(F32)
        o_ref[:, gs] = _rms_scale(gated, nw_ref[:, gs]).astype(BF16)

    def group_batch_body(i, carry):
        for slot in range(GROUP_SLOTS):
            one_group(GROUP_SLOTS * i + slot, slot)
        return carry

    lax.fori_loop(0, N_GROUPS // GROUP_SLOTS, group_batch_body, 0)

    @pl.when(ti == nt - 1)
    def _():
        for g in range(N_GROUPS):
            ns_ref[0, g * HEADS_PER_GROUP:(g + 1) * HEADS_PER_GROUP] = (
                ht_ref[g].T.reshape(HEADS_PER_GROUP, HEAD_DIM, D_STATE))


def _ssd_branch(pall, dt, st_conv, st_ssm, conv_w, conv_b, alog_row, dskip_row, norm_w, sel, *, ls, b, l):
    lv = min(l, SSD_Q)
    nt = l // lv
    m = b * l
    return pl.pallas_call(
        functools.partial(_ssd_kernel, lv=lv, nt=nt),
        grid=(b, nt),
        in_specs=[
            pl.BlockSpec((lv, D_CONV), lambda bi, ti: (bi * nt + ti, COL_XBC // D_CONV)),
            pl.BlockSpec((lv, D_INNER), lambda bi, ti: (bi * nt + ti, COL_Z // D_INNER)),
            pl.BlockSpec((lv, DT_LANES), lambda bi, ti: (bi * nt + ti, 0)),
            pl.BlockSpec((1, CONV_HALO, D_CONV), lambda bi, ti: (bi, 0, 0)),
            pl.BlockSpec((None, 1, N_HEADS, HEAD_DIM, D_STATE), lambda bi, ti: (ls, bi, 0, 0, 0)),
            pl.BlockSpec((CONV_W, D_CONV), lambda bi, ti: (0, 0)),
            pl.BlockSpec((1, D_CONV), lambda bi, ti: (0, 0)),
            pl.BlockSpec((1, DT_LANES), lambda bi, ti: (0, 0)),
            pl.BlockSpec((1, D_INNER), lambda bi, ti: (0, 0)),
            pl.BlockSpec((1, D_INNER), lambda bi, ti: (0, 0)),
            pl.BlockSpec((SEL_ROWS, SEL_COLS), lambda bi, ti: (0, 0)),
        ],
        out_specs=[
            pl.BlockSpec((lv, D_INNER), lambda bi, ti: (bi * nt + ti, 0)),
            pl.BlockSpec((1, N_HEADS, HEAD_DIM, D_STATE), lambda bi, ti: (bi, 0, 0, 0)),
        ],
        out_shape=[
            jax.ShapeDtypeStruct((m, D_INNER), BF16),
            jax.ShapeDtypeStruct((b, N_HEADS, HEAD_DIM, D_STATE), F32),
        ],
        scratch_shapes=[
            pltpu.VMEM((CONV_HALO + SSD_Q, D_CONV), F32),
            pltpu.VMEM((SSD_Q, D_CONV), F32),
            pltpu.VMEM((N_GROUPS, D_STATE, GROUP_W), F32),
            pltpu.VMEM((DT_LANES, SSD_Q), F32),
            pltpu.VMEM((DT_LANES, SSD_Q), F32),
            pltpu.VMEM((GROUP_SLOTS, SSD_Q, SEL_COLS), F32),
            pltpu.VMEM((GROUP_SLOTS, SSD_Q, GROUP_W), F32),
            pltpu.VMEM((GROUP_SLOTS, SSD_Q, GROUP_W), F32),
        ],
        compiler_params=_params(2),
        name="ssd_branch",
    )(pall, pall, dt, st_conv, st_ssm, conv_w, conv_b, alog_row, dskip_row, norm_w, sel)


def _mix_out_kernel(a_ref, ws_ref, bp_ref, g1_ref, wo_ref, gp_ref, x_ref, o_ref):
    br_ssd = _dot(a_ref[...], ws_ref[...])
    merged = (bp_ref[...].astype(F32) + g1_ref[...].astype(F32) * br_ssd).astype(BF16)
    out = _dot(merged, wo_ref[...])
    o_ref[...] = x_ref[...] + _rms_scale(out, gp_ref[...])


def _mix_out(a_ssd, w_ssd, bp, pall, w_out, g_post, x, *, li, tm):
    m = x.shape[0]
    resident = pl.Buffered(1)
    return pl.pallas_call(
        _mix_out_kernel,
        grid=(m // tm,),
        in_specs=[
            pl.BlockSpec((tm, D_INNER), lambda i: (i, 0)),
            pl.BlockSpec((None, D_INNER, D_MODEL), lambda i: (li, 0, 0), pipeline_mode=resident),
            pl.BlockSpec((tm, D_MODEL), lambda i: (i, 0)),
            pl.BlockSpec((tm, D_MODEL), lambda i: (i, COL_G1 // D_MODEL)),
            pl.BlockSpec((None, D_MODEL, D_MODEL), lambda i: (li, 0, 0), pipeline_mode=resident),
            pl.BlockSpec((1, D_MODEL), lambda i: (0, 0)),
            pl.BlockSpec((tm, D_MODEL), lambda i: (i, 0)),
        ],
        out_specs=pl.BlockSpec((tm, D_MODEL), lambda i: (i, 0)),
        out_shape=jax.ShapeDtypeStruct((m, D_MODEL), F32),
        compiler_params=_params(1),
        name="mix_out",
    )(a_ssd, w_ssd, bp, pall, w_out, g_post, x)


def _mlp_kernel(x_ref, gpre_ref, wu_ref, wd_ref, gpost_ref, o_ref, u_ref, acc_ref, *, nf):
    j = pl.program_id(1)

    @pl.when(j == 0)
    def _():
        u_ref[...] = _rms_scale(x_ref[...], gpre_ref[...]).astype(BF16)
        acc_ref[...] = jnp.zeros_like(acc_ref)

    hid = jnp.maximum(_dot(u_ref[...], wu_ref[...]), 0.0)
    acc_ref[...] += _dot((hid * hid).astype(BF16), wd_ref[...])

    @pl.when(j == nf - 1)
    def _():
        o_ref[...] = x_ref[...] + _rms_scale(acc_ref[...], gpost_ref[...])


def _mlp(x, g_pre, w_up, w_down, g_post, *, li, tm, tf):
    m = x.shape[0]
    nf = D_FF // tf
    return pl.pallas_call(
        functools.partial(_mlp_kernel, nf=nf),
        grid=(m // tm, nf),
        in_specs=[
            pl.BlockSpec((tm, D_MODEL), lambda i, j: (i, 0)),
            pl.BlockSpec((1, D_MODEL), lambda i, j: (0, 0)),
            pl.BlockSpec((None, D_MODEL, tf), lambda i, j: (li, 0, j)),
            pl.BlockSpec((None, tf, D_MODEL), lambda i, j: (li, j, 0)),
            pl.BlockSpec((1, D_MODEL), lambda i, j: (0, 0)),
        ],
        out_specs=pl.BlockSpec((tm, D_MODEL), lambda i, j: (i, 0)),
        out_shape=jax.ShapeDtypeStruct((m, D_MODEL), F32),
        scratch_shapes=[pltpu.VMEM((tm, D_MODEL), BF16), pltpu.VMEM((tm, D_MODEL), F32)],
        compiler_params=_params(2),
        name="mlp",
    )(x, g_pre, w_up, w_down, g_post)


def _tiles(m):
    return min(m, 512)


def _layer_group(x, st_pool, st_conv, st_ssm, pos0, p, big, *, li, ls, b, l):
    m = b * l
    tm = _tiles(m)
    pall, dt = _in_proj(x, p["g_mix_pre"], big["w_in"], p["w_dt"], p["dt_bias"], li=li, tm=tm, tn=2048)

    st16 = jnp.pad(st_pool, ((0, 0), (POOL_HALO - POOL_KEEP, 0), (0, 0)))
    st_hi = st16.astype(BF16)
    st_lo = (st16 - st_hi.astype(F32)).astype(BF16)
    bp = _pool_branch(pall, st_hi, st_lo, big["pool_w"], p["pool_scale"], big["w_pool_proj"],
                      li=li, b=b, l=l, lt=min(l, 256), pos0=pos0)

    stc = jnp.pad(st_conv, ((0, 0), (CONV_HALO - (CONV_W - 1), 0), (0, 0)))
    a_ssd, new_ssm = _ssd_branch(pall, dt, stc, st_ssm, p["conv_w"], p["conv_b"], p["a_log"], p["d_skip"],
                                 p["ssd_norm"], _lane_selector(), ls=ls, b=b, l=l)

    x = _mix_out(a_ssd, big["w_ssd_proj"], bp, pall, big["w_out"], p["g_mix_post"], x, li=li, tm=min(m, 256))
    x = _mlp(x, p["g_mlp_pre"], big["w_up"], big["w_down"], p["g_mlp_post"], li=li, tm=tm, tf=1024)

    p3 = pall.reshape(b, l, P_COLS)
    new_pool = p3[:, l - POOL_KEEP:, COL_POOL:COL_POOL + D_MODEL].astype(F32)
    new_conv = p3[:, l - (CONV_W - 1):, COL_XBC:COL_XBC + D_CONV].astype(F32)
    return x, new_pool, new_conv, new_ssm


def _heads_to_lanes(v):
    lead = v.shape[:-1]
    v = v.reshape(lead + (N_GROUPS, HEADS_PER_GROUP))
    v = jnp.pad(v, [(0, 0)] * len(lead) + [(0, 0), (0, HEAD_LANE_STRIDE - HEADS_PER_GROUP)])
    return v.reshape(lead + (DT_LANES,))


def _lane_selector():
    j = lax.broadcasted_iota(jnp.int32, (SEL_ROWS, SEL_COLS), 0)
    c = lax.broadcasted_iota(jnp.int32, (SEL_ROWS, SEL_COLS), 1)
    blk = c // LANE
    src = jnp.where(blk < HEADS_PER_GROUP, blk, 2 * (blk - HEADS_PER_GROUP) + (c % LANE) // HEAD_DIM)
    return jnp.where(j == SEL_ROW0 + src, 1.0, 0.0).astype(BF16)


def _prep_big(w_in, pool_w, w_pool_proj, w_ssd_proj, w_out, w_up, w_down):
    return {
        "w_in": jnp.swapaxes(w_in, 1, 2).astype(BF16),
        "pool_w": pool_w.astype(BF16),
        "w_pool_proj": w_pool_proj.astype(BF16),
        "w_ssd_proj": w_ssd_proj.astype(BF16),
        "w_out": w_out.astype(BF16),
        "w_up": w_up.astype(BF16),
        "w_down": w_down.astype(BF16),
    }


def _prep_layer(wt_dt, pool_scale, conv_w, conv_b, dt_bias, a_log, d_skip, ssd_norm, g_mix_pre, g_mix_post,
                g_mlp_pre, g_mlp_post):
    w_dt = wt_dt.reshape(N_GROUPS, HEADS_PER_GROUP, D_MODEL)
    w_dt = jnp.pad(w_dt, ((0, 0), (0, HEAD_LANE_STRIDE - HEADS_PER_GROUP), (0, 0))).reshape(DT_LANES, D_MODEL)
    row = lambda v: v.reshape(1, -1)
    return {
        "w_dt": w_dt,
        "dt_bias": _heads_to_lanes(row(dt_bias)),
        "pool_scale": row(pool_scale),
        "conv_w": conv_w,
        "conv_b": row(conv_b),
        "a_log": _heads_to_lanes(row(a_log)),
        "d_skip": row(jnp.repeat(d_skip, HEAD_DIM)),
        "ssd_norm": row(ssd_norm),
        "g_mix_pre": row(g_mix_pre),
        "g_mix_post": row(g_mix_post),
        "g_mlp_pre": row(g_mlp_pre),
        "g_mlp_post": row(g_mlp_post),
    }


@jax.jit
def kernel(x_prompt, x_sample, state_pool, state_conv, state_ssm, w_in, pool_w, pool_scale, conv_w, conv_b,
           dt_bias, a_log, d_skip, ssd_norm, w_pool_proj, w_ssd_proj, w_out, g_mix_pre, g_mix_post,
           w_up, w_down, g_mlp_pre, g_mlp_post):
    bp, lp, _ = x_prompt.shape
    bs, ls, _ = x_sample.shape
    depth = w_in.shape[0]
    past_len = 1024
    zero_pool = jnp.zeros((bp, POOL_KEEP, D_MODEL), F32)
    zero_conv = jnp.zeros((bp, CONV_W - 1, D_CONV), F32)
    zero_ssm = jnp.zeros((1, bp, N_HEADS, HEAD_DIM, D_STATE), F32)
    yp = x_prompt.reshape(bp * lp, D_MODEL)
    ys = x_sample.reshape(bs * ls, D_MODEL)
    big = _prep_big(w_in, pool_w, w_pool_proj, w_ssd_proj, w_out, w_up, w_down)
    outs = [[] for _ in range(6)]
    for li in range(depth):
        wt_dt = lax.slice(big["w_in"], (li, SRC_COL_DT, 0), (li + 1, SRC_COL_GATE, D_MODEL))[0]
        p = _prep_layer(wt_dt, pool_scale[li], conv_w[li], conv_b[li], dt_bias[li], a_log[li], d_skip[li],
                        ssd_norm[li], g_mix_pre[li], g_mix_post[li], g_mlp_pre[li], g_mlp_post[li])
        yp, pool_p, conv_p, ssm_p = _layer_group(yp, zero_pool, zero_conv, zero_ssm, 0, p, big,
                                                 li=li, ls=0, b=bp, l=lp)
        ys, pool_s, conv_s, ssm_s = _layer_group(ys, state_pool[li], state_conv[li], state_ssm, past_len, p, big,
                                                 li=li, ls=li, b=bs, l=ls)
        for lst, v in zip(outs, (pool_p, conv_p, ssm_p, pool_s, conv_s, ssm_s)):
            lst.append(v)
    return (yp.reshape(bp, lp, D_MODEL), ys.reshape(bs, ls, D_MODEL)) + tuple(jnp.stack(v) for v in outs)
```

```python
import functools

import jax
import jax.numpy as jnp
from jax import lax
from jax.experimental import pallas as pl
from jax.experimental.pallas import tpu as pltpu

F32 = jnp.float32
BF16 = jnp.bfloat16

D_MODEL = 2048
EPS = 1e-6
POOL_WINDOWS = (2, 4, 8, 16)
POOL_GROUP = D_MODEL // len(POOL_WINDOWS)
POOL_KEEP = max(POOL_WINDOWS) - 1
POOL_HALO = 16
D_INNER = 2 * D_MODEL
HEAD_DIM = 64
N_HEADS = D_INNER // HEAD_DIM
N_GROUPS = 8
HEADS_PER_GROUP = N_HEADS // N_GROUPS
GROUP_W = D_INNER // N_GROUPS
D_STATE = 128
CONV_W = 4
D_CONV = D_INNER + 2 * N_GROUPS * D_STATE
SSD_Q = 128
GROUP_SLOTS = 4
CONV_HALO = 8
D_FF = 4 * D_MODEL
COL_XBC = 0
COL_POOL = COL_XBC + D_CONV
COL_Z = COL_POOL + D_MODEL
COL_G0 = COL_Z + D_INNER
COL_G1 = COL_G0 + D_MODEL
P_COLS = COL_G1 + D_MODEL
SRC_COL_POOL = 0
SRC_COL_XBC = SRC_COL_POOL + D_MODEL + D_INNER
SRC_COL_DT = SRC_COL_XBC + D_CONV
SRC_COL_GATE = SRC_COL_DT + N_HEADS
IN_COLS = SRC_COL_GATE + 2 * D_MODEL
DT_LANES = 128
HEAD_LANE_STRIDE = DT_LANES // N_GROUPS
SEL_ROW0 = DT_LANES - HEAD_LANE_STRIDE
SEL_ROWS = SEL_ROW0 + DT_LANES
SEL_COLS = (HEADS_PER_GROUP + HEADS_PER_GROUP // 2) * 128
NEG_BIG = -1e30
LOG2E = 1.4426950408889634
LANE = 128
VMEM_LIMIT = 56 * 1024 * 1024


def _params(n_axes):
    return pltpu.CompilerParams(dimension_semantics=("arbitrary",) * n_axes, vmem_limit_bytes=VMEM_LIMIT)


def _sigmoid(v):
    return 1.0 / (1.0 + jnp.exp2(v * (-LOG2E)))


def _rms_scale(v, g):
    ms = jnp.mean(v * v, axis=-1, keepdims=True)
    return v * lax.rsqrt(ms + EPS) * g


def _dot(a, b):
    return jnp.dot(a, b, preferred_element_type=F32)


def _dot_nt(a, b):
    return lax.dot_general(a, b, (((1,), (1,)), ((), ())), preferred_element_type=F32)


def _in_proj_kernel(x_ref, g_ref, w_ref, wdt_ref, dtb_ref, p_ref, dt_ref, u_ref, *, tn):
    j = pl.program_id(1)

    @pl.when(j == 0)
    def _():
        u = _rms_scale(x_ref[...], g_ref[...]).astype(BF16)
        u_ref[...] = u
        raw = _dot_nt(u, wdt_ref[...]) + dtb_ref[...]
        dt_ref[...] = jnp.maximum(raw, 0.0) + jnp.log1p(jnp.exp(-jnp.abs(raw)))

    col = j * tn

    @pl.when(col < COL_Z)
    def _():
        p_ref[...] = _dot_nt(u_ref[...], w_ref[0]).astype(BF16)

    @pl.when(col >= COL_Z)
    def _():
        acc = _dot_nt(u_ref[...], w_ref[0])
        sig = _sigmoid(acc)
        p_ref[...] = jnp.where(col >= COL_G0, sig, acc * sig).astype(BF16)


def _in_proj(x, g, wt, wdt, dtb, *, li, tm, tn):
    m = x.shape[0]
    n_x = D_CONV // tn
    n_plain = n_x + (D_MODEL + D_INNER) // tn

    unit = N_HEADS

    def w_rows(i, j):
        plain = jnp.where(j < n_x, (SRC_COL_XBC + j * tn) // unit, (SRC_COL_POOL + (j - n_x) * tn) // unit)
        return (li, jnp.where(j < n_plain, plain, (SRC_COL_GATE + (j - n_plain) * tn) // unit) * unit, 0)

    return pl.pallas_call(
        functools.partial(_in_proj_kernel, tn=tn),
        grid=(m // tm, P_COLS // tn),
        in_specs=[
            pl.BlockSpec((tm, D_MODEL), lambda i, j: (i, 0), pipeline_mode=pl.Buffered(1)),
            pl.BlockSpec((1, D_MODEL), lambda i, j: (0, 0)),
            pl.BlockSpec((pl.Element(1), pl.Element(tn), pl.Element(D_MODEL)), w_rows),
            pl.BlockSpec((DT_LANES, D_MODEL), lambda i, j: (0, 0)),
            pl.BlockSpec((1, DT_LANES), lambda i, j: (0, 0)),
        ],
        out_specs=[
            pl.BlockSpec((tm, tn), lambda i, j: (i, j)),
            pl.BlockSpec((tm, DT_LANES), lambda i, j: (i, 0)),
        ],
        out_shape=[jax.ShapeDtypeStruct((m, P_COLS), BF16), jax.ShapeDtypeStruct((m, DT_LANES), F32)],
        scratch_shapes=[pltpu.VMEM((tm, D_MODEL), BF16)],
        compiler_params=_params(2),
        name="in_proj",
    )(x, g, wt, wdt, dtb)


def _pool_kernel(p_ref, g0_ref, sth_ref, stl_ref, pw_ref, ps_ref, wpp_ref, o_ref, hh_ref, hl_ref, a_ref,
                 *, lt, pos0):
    ti = pl.program_id(1)

    @pl.when(ti == 0)
    def _():
        hh_ref[...] = sth_ref[0]
        hl_ref[...] = stl_ref[0]

    cur = p_ref[...]
    diff = lax.broadcasted_iota(jnp.int32, (lt, lt), 0) - lax.broadcasted_iota(jnp.int32, (lt, lt), 1)
    dist = (lax.broadcasted_iota(jnp.int32, (lt, POOL_HALO), 0) + POOL_HALO
            - lax.broadcasted_iota(jnp.int32, (lt, POOL_HALO), 1))
    pos = pos0 + ti * lt + lax.broadcasted_iota(jnp.int32, (lt, 1), 0)
    for g, w in enumerate(POOL_WINDOWS):
        cs = slice(g * POOL_GROUP, (g + 1) * POOL_GROUP)
        band = jnp.where(diff >= 0, jnp.where(diff < w, 1.0, 0.0), 0.0).astype(BF16)
        hband = jnp.where(dist < w, 1.0, 0.0).astype(BF16)
        xg = cur[:, cs]
        halo = jnp.concatenate([hh_ref[:, cs], hl_ref[:, cs]], axis=0)
        win = _dot(band, xg) + _dot(jnp.concatenate([hband, hband], axis=1), halo)
        cnt = jnp.minimum(w, pos + 1).astype(F32)
        d = win * (1.0 / cnt) - xg.astype(F32)
        a_ref[:, cs] = (_dot(d.astype(BF16), pw_ref[g]) * ps_ref[:, cs]).astype(BF16)
    o_ref[...] = (_dot(a_ref[...], wpp_ref[...]) * g0_ref[...].astype(F32)).astype(BF16)
    hh_ref[...] = cur[lt - POOL_HALO:, :]
    hl_ref[...] = jnp.zeros_like(hl_ref)


def _pool_branch(pall, st_hi, st_lo, pool_w, pool_scale, wpp, *, li, b, l, lt, pos0):
    nt = l // lt
    m = b * l
    return pl.pallas_call(
        functools.partial(_pool_kernel, lt=lt, pos0=pos0),
        grid=(b, nt),
        in_specs=[
            pl.BlockSpec((lt, D_MODEL), lambda bi, ti: (bi * nt + ti, COL_POOL // D_MODEL)),
            pl.BlockSpec((lt, D_MODEL), lambda bi, ti: (bi * nt + ti, COL_G0 // D_MODEL)),
            pl.BlockSpec((1, POOL_HALO, D_MODEL), lambda bi, ti: (bi, 0, 0)),
            pl.BlockSpec((1, POOL_HALO, D_MODEL), lambda bi, ti: (bi, 0, 0)),
            pl.BlockSpec((None, len(POOL_WINDOWS), POOL_GROUP, POOL_GROUP), lambda bi, ti: (li, 0, 0, 0)),
            pl.BlockSpec((1, D_MODEL), lambda bi, ti: (0, 0)),
            pl.BlockSpec((None, D_MODEL, D_MODEL), lambda bi, ti: (li, 0, 0)),
        ],
        out_specs=pl.BlockSpec((lt, D_MODEL), lambda bi, ti: (bi * nt + ti, 0)),
        out_shape=jax.ShapeDtypeStruct((m, D_MODEL), BF16),
        scratch_shapes=[
            pltpu.VMEM((POOL_HALO, D_MODEL), BF16),
            pltpu.VMEM((POOL_HALO, D_MODEL), BF16),
            pltpu.VMEM((lt, D_MODEL), BF16),
        ],
        compiler_params=_params(2),
        name="pool_branch",
    )(pall, pall, st_hi, st_lo, pool_w, pool_scale, wpp)


def _split3(v):
    hi = v.astype(BF16)
    r1 = v - hi.astype(F32)
    mid = r1.astype(BF16)
    lo = (r1 - mid.astype(F32)).astype(BF16)
    return hi, mid, lo


def _ssd_kernel(xbc_ref, zs_ref, dt_ref, stc_ref, sts_ref, cw_ref, cb_ref, alog_ref, dsk_ref, nw_ref, sel_ref,
                o_ref, ns_ref, xc_ref, xa_ref, ht_ref, dtt_ref, cumt_ref, colb2_ref, yst2_ref, y2_ref, *, lv, nt):
    q = SSD_Q
    ti = pl.program_id(1)

    @pl.when(ti == 0)
    def _():
        xc_ref[0:CONV_HALO, :] = stc_ref[0]
        for g in range(N_GROUPS):
            sg = sts_ref[0, g * HEADS_PER_GROUP:(g + 1) * HEADS_PER_GROUP]
            ht_ref[g] = sg.reshape(GROUP_W, D_STATE).T

    if lv < q:
        xa_ref[lv:q, :] = jnp.zeros((q - lv, D_CONV), F32)
    cwid = 256

    def conv_body(c, carry):
        cs = pl.ds(pl.multiple_of(c * cwid, cwid), cwid)
        xc_ref[CONV_HALO:CONV_HALO + lv, cs] = xbc_ref[:, cs].astype(F32)
        full = xc_ref[0:CONV_HALO + lv, cs]
        acc = full[CONV_HALO:] * cw_ref[CONV_W - 1:CONV_W, cs] + cb_ref[:, cs]
        for k in range(1, CONV_W):
            tap = CONV_W - 1 - k
            acc = acc + pltpu.roll(full, k, axis=0)[CONV_HALO:] * cw_ref[tap:tap + 1, cs]
        xa_ref[0:lv, cs] = acc * _sigmoid(acc)
        return carry

    lax.fori_loop(0, D_CONV // cwid, conv_body, 0)
    xc_ref[0:CONV_HALO, :] = xc_ref[lv:lv + CONV_HALO, :]

    if lv < q:
        dt_tm = jnp.concatenate([dt_ref[...], jnp.zeros((q - lv, DT_LANES), F32)], axis=0)
    else:
        dt_tm = dt_ref[...]
    a2 = dt_tm * (-LOG2E * jnp.exp(alog_ref[...]))
    row_i = lax.broadcasted_iota(jnp.int32, (q, q), 0)
    col_i = lax.broadcasted_iota(jnp.int32, (q, q), 1)
    causal = row_i >= col_i
    lower = jnp.where(causal, 1.0, 0.0).astype(BF16)
    hi, mid, lo = _split3(a2)
    cum = _dot(lower, hi) + _dot(lower, mid) + _dot(lower, lo)
    c2 = jnp.concatenate(_split3(cum)[0:2], axis=1)
    cumt_ref[...] = cum.T
    dtt_ref[...] = dt_tm.T
    lane_row = lax.broadcasted_iota(jnp.int32, (1, LANE), 1)
    half0 = jnp.where(lane_row < HEAD_DIM, 1.0, 0.0).astype(BF16)
    half1 = jnp.where(lane_row >= HEAD_DIM, 1.0, 0.0).astype(BF16)

    def one_group(g, slot):
        colb_ref, yst_ref, y_ref = colb2_ref.at[slot], yst2_ref.at[slot], y2_ref.at[slot]
        go = g * GROUP_W
        bg = xa_ref[:, pl.ds(pl.multiple_of(D_INNER + g * D_STATE, D_STATE), D_STATE)]
        cg = xa_ref[:, pl.ds(pl.multiple_of(D_INNER + N_GROUPS * D_STATE + g * D_STATE, D_STATE), D_STATE)]
        cg16 = cg.astype(BF16)
        cb = lax.dot_general(cg16, bg.astype(BF16), (((1,), (1,)), ((), ())), preferred_element_type=F32)
        bt = bg.T
        selg = sel_ref[pl.ds(pl.multiple_of(SEL_ROW0 - HEAD_LANE_STRIDE * g, HEAD_LANE_STRIDE), DT_LANES), :]
        colb_ref[...] = _dot(c2, jnp.concatenate([selg, selg], axis=0))
        yst_ref[...] = _dot(cg16, ht_ref[g].astype(BF16))
        for j in range(HEADS_PER_GROUP // 2):
            pair = slice(j * LANE, (j + 1) * LANE)
            mixes = []
            bws = []
            lasts = []
            for e in range(2):
                r = 2 * j + e
                h = g * HEAD_LANE_STRIDE + r
                cum_row = cumt_ref[pl.ds(h, 1), :]
                last = cum_row[:, q - 1:q]
                src_row = cum_row - jnp.log2(dtt_ref[pl.ds(h, 1), :])
                cum_col = colb_ref[:, r * LANE:(r + 1) * LANE]
                mixes.append((cb * jnp.exp2(jnp.where(causal, cum_col - src_row, NEG_BIG))).astype(BF16))
                bws.append((bt * jnp.exp2(last - src_row)).astype(BF16))
                lasts.append(jnp.exp2(last))
            xs = xa_ref[:, pl.ds(pl.multiple_of(go + j * LANE, LANE), LANE)]
            xs16 = xs.astype(BF16)
            rhs = jnp.concatenate([xs16 * half0, xs16 * half1], axis=0)
            dsk = dsk_ref[:, pl.ds(pl.multiple_of(go + j * LANE, LANE), LANE)]
            cum_pair = colb_ref[:, (HEADS_PER_GROUP + j) * LANE:(HEADS_PER_GROUP + j + 1) * LANE]
            y_ref[:, pair] = (_dot(jnp.concatenate(mixes, axis=1), rhs)
                              + yst_ref[:, pair] * jnp.exp2(cum_pair) + dsk * xs)
            keep = jnp.where(lane_row < HEAD_DIM, lasts[0], lasts[1])
            ht_ref[g, :, pair] = ht_ref[g, :, pair] * keep + _dot(jnp.concatenate(bws, axis=1), rhs)
        gs = pl.ds(pl.multiple_of(go, GROUP_W), GROUP_W)
        gated = y_ref[0:lv, :] * zs_ref[:, gs].astype(F32)
        o_ref[:, gs] = _rms_scale(gated, nw_ref[:, gs]).astype(BF16)

    def group_batch_body(i, carry):
        for slot in range(GROUP_SLOTS):
            one_group(GROUP_SLOTS * i + slot, slot)
        return carry

    lax.fori_loop(0, N_GROUPS // GROUP_SLOTS, group_batch_body, 0)

    @pl.when(ti == nt - 1)
    def _():
        for g in range(N_GROUPS):
            ns_ref[0, g * HEADS_PER_GROUP:(g + 1) * HEADS_PER_GROUP] = (
                ht_ref[g].T.reshape(HEADS_PER_GROUP, HEAD_DIM, D_STATE))


def _ssd_branch(pall, dt, st_conv, st_ssm, conv_w, conv_b, alog_row, dskip_row, norm_w, sel, *, ls, b, l):
    lv = min(l, SSD_Q)
    nt = l // lv
    m = b * l
    return pl.pallas_call(
        functools.partial(_ssd_kernel, lv=lv, nt=nt),
        grid=(b, nt),
        in_specs=[
            pl.BlockSpec((lv, D_CONV), lambda bi, ti: (bi * nt + ti, COL_XBC // D_CONV)),
            pl.BlockSpec((lv, D_INNER), lambda bi, ti: (bi * nt + ti, COL_Z // D_INNER)),
            pl.BlockSpec((lv, DT_LANES), lambda bi, ti: (bi * nt + ti, 0)),
            pl.BlockSpec((1, CONV_HALO, D_CONV), lambda bi, ti: (bi, 0, 0)),
            pl.BlockSpec((None, 1, N_HEADS, HEAD_DIM, D_STATE), lambda bi, ti: (ls, bi, 0, 0, 0)),
            pl.BlockSpec((CONV_W, D_CONV), lambda bi, ti: (0, 0)),
            pl.BlockSpec((1, D_CONV), lambda bi, ti: (0, 0)),
            pl.BlockSpec((1, DT_LANES), lambda bi, ti: (0, 0)),
            pl.BlockSpec((1, D_INNER), lambda bi, ti: (0, 0)),
            pl.BlockSpec((1, D_INNER), lambda bi, ti: (0, 0)),
            pl.BlockSpec((SEL_ROWS, SEL_COLS), lambda bi, ti: (0, 0)),
        ],
        out_specs=[
            pl.BlockSpec((lv, D_INNER), lambda bi, ti: (bi * nt + ti, 0)),
            pl.BlockSpec((1, N_HEADS, HEAD_DIM, D_STATE), lambda bi, ti: (bi, 0, 0, 0)),
        ],
        out_shape=[
            jax.ShapeDtypeStruct((m, D_INNER), BF16),
            jax.ShapeDtypeStruct((b, N_HEADS, HEAD_DIM, D_STATE), F32),
        ],
        scratch_shapes=[
            pltpu.VMEM((CONV_HALO + SSD_Q, D_CONV), F32),
            pltpu.VMEM((SSD_Q, D_CONV), F32),
            pltpu.VMEM((N_GROUPS, D_STATE, GROUP_W), F32),
            pltpu.VMEM((DT_LANES, SSD_Q), F32),
            pltpu.VMEM((DT_LANES, SSD_Q), F32),
            pltpu.VMEM((GROUP_SLOTS, SSD_Q, SEL_COLS), F32),
            pltpu.VMEM((GROUP_SLOTS, SSD_Q, GROUP_W), F32),
            pltpu.VMEM((GROUP_SLOTS, SSD_Q, GROUP_W), F32),
        ],
        compiler_params=_params(2),
        name="ssd_branch",
    )(pall, pall, dt, st_conv, st_ssm, conv_w, conv_b, alog_row, dskip_row, norm_w, sel)


def _mix_out_kernel(a_ref, ws_ref, bp_ref, g1_ref, wo_ref, gp_ref, x_ref, o_ref):
    br_ssd = _dot(a_ref[...], ws_ref[...])
    merged = (bp_ref[...].astype(F32) + g1_ref[...].astype(F32) * br_ssd).astype(BF16)
    out = _dot(merged, wo_ref[...])
    o_ref[...] = x_ref[...] + _rms_scale(out, gp_ref[...])


def _mix_out(a_ssd, w_ssd, bp, pall, w_out, g_post, x, *, li, tm):
    m = x.shape[0]
    resident = pl.Buffered(1)
    return pl.pallas_call(
        _mix_out_kernel,
        grid=(m // tm,),
        in_specs=[
            pl.BlockSpec((tm, D_INNER), lambda i: (i, 0)),
            pl.BlockSpec((None, D_INNER, D_MODEL), lambda i: (li, 0, 0), pipeline_mode=resident),
            pl.BlockSpec((tm, D_MODEL), lambda i: (i, 0)),
            pl.BlockSpec((tm, D_MODEL), lambda i: (i, COL_G1 // D_MODEL)),
            pl.BlockSpec((None, D_MODEL, D_MODEL), lambda i: (li, 0, 0), pipeline_mode=resident),
            pl.BlockSpec((1, D_MODEL), lambda i: (0, 0)),
            pl.BlockSpec((tm, D_MODEL), lambda i: (i, 0)),
        ],
        out_specs=pl.BlockSpec((tm, D_MODEL), lambda i: (i, 0)),
        out_shape=jax.ShapeDtypeStruct((m, D_MODEL), F32),
        compiler_params=_params(1),
        name="mix_out",
    )(a_ssd, w_ssd, bp, pall, w_out, g_post, x)


def _mlp_kernel(x_ref, gpre_ref, wu_ref, wd_ref, gpost_ref, o_ref, u_ref, acc_ref, *, nf):
    j = pl.program_id(1)

    @pl.when(j == 0)
    def _():
        u_ref[...] = _rms_scale(x_ref[...], gpre_ref[...]).astype(BF16)
        acc_ref[...] = jnp.zeros_like(acc_ref)

    hid = jnp.maximum(_dot(u_ref[...], wu_ref[...]), 0.0)
    acc_ref[...] += _dot((hid * hid).astype(BF16), wd_ref[...])

    @pl.when(j == nf - 1)
    def _():
        o_ref[...] = x_ref[...] + _rms_scale(acc_ref[...], gpost_ref[...])


def _mlp(x, g_pre, w_up, w_down, g_post, *, li, tm, tf):
    m = x.shape[0]
    nf = D_FF // tf
    return pl.pallas_call(
        functools.partial(_mlp_kernel, nf=nf),
        grid=(m // tm, nf),
        in_specs=[
            pl.BlockSpec((tm, D_MODEL), lambda i, j: (i, 0)),
            pl.BlockSpec((1, D_MODEL), lambda i, j: (0, 0)),
            pl.BlockSpec((None, D_MODEL, tf), lambda i, j: (li, 0, j)),
            pl.BlockSpec((None, tf, D_MODEL), lambda i, j: (li, j, 0)),
            pl.BlockSpec((1, D_MODEL), lambda i, j: (0, 0)),
        ],
        out_specs=pl.BlockSpec((tm, D_MODEL), lambda i, j: (i, 0)),
        out_shape=jax.ShapeDtypeStruct((m, D_MODEL), F32),
        scratch_shapes=[pltpu.VMEM((tm, D_MODEL), BF16), pltpu.VMEM((tm, D_MODEL), F32)],
        compiler_params=_params(2),
        name="mlp",
    )(x, g_pre, w_up, w_down, g_post)


def _tiles(m):
    return min(m, 512)


def _layer_group(x, st_pool, st_conv, st_ssm, pos0, p, big, *, li, ls, b, l):
    m = b * l
    tm = _tiles(m)
    pall, dt = _in_proj(x, p["g_mix_pre"], big["w_in"], p["w_dt"], p["dt_bias"], li=li, tm=min(m, 1024), tn=2048)

    st16 = jnp.pad(st_pool, ((0, 0), (POOL_HALO - POOL_KEEP, 0), (0, 0)))
    st_hi = st16.astype(BF16)
    st_lo = (st16 - st_hi.astype(F32)).astype(BF16)
    bp = _pool_branch(pall, st_hi, st_lo, big["pool_w"], p["pool_scale"], big["w_pool_proj"],
                      li=li, b=b, l=l, lt=min(l, 256), pos0=pos0)

    stc = jnp.pad(st_conv, ((0, 0), (CONV_HALO - (CONV_W - 1), 0), (0, 0)))
    a_ssd, new_ssm = _ssd_branch(pall, dt, stc, st_ssm, p["conv_w"], p["conv_b"], p["a_log"], p["d_skip"],
                                 p["ssd_norm"], _lane_selector(), ls=ls, b=b, l=l)

    x = _mix_out(a_ssd, big["w_ssd_proj"], bp, pall, big["w_out"], p["g_mix_post"], x, li=li, tm=min(m, 256))
    x = _mlp(x, p["g_mlp_pre"], big["w_up"], big["w_down"], p["g_mlp_post"], li=li, tm=tm, tf=1024)

    p3 = pall.reshape(b, l, P_COLS)
    new_pool = p3[:, l - POOL_KEEP:, COL_POOL:COL_POOL + D_MODEL].astype(F32)
    new_conv = p3[:, l - (CONV_W - 1):, COL_XBC:COL_XBC + D_CONV].astype(F32)
    return x, new_pool, new_conv, new_ssm


def _heads_to_lanes(v):
    lead = v.shape[:-1]
    v = v.reshape(lead + (N_GROUPS, HEADS_PER_GROUP))
    v = jnp.pad(v, [(0, 0)] * len(lead) + [(0, 0), (0, HEAD_LANE_STRIDE - HEADS_PER_GROUP)])
    return v.reshape(lead + (DT_LANES,))


def _lane_selector():
    j = lax.broadcasted_iota(jnp.int32, (SEL_ROWS, SEL_COLS), 0)
    c = lax.broadcasted_iota(jnp.int32, (SEL_ROWS, SEL_COLS), 1)
    blk = c // LANE
    src = jnp.where(blk < HEADS_PER_GROUP, blk, 2 * (blk - HEADS_PER_GROUP) + (c % LANE) // HEAD_DIM)
    return jnp.where(j == SEL_ROW0 + src, 1.0, 0.0).astype(BF16)


def _prep_big(w_in, pool_w, w_pool_proj, w_ssd_proj, w_out, w_up, w_down):
    return {
        "w_in": jnp.swapaxes(w_in, 1, 2).astype(BF16),
        "pool_w": pool_w.astype(BF16),
        "w_pool_proj": w_pool_proj.astype(BF16),
        "w_ssd_proj": w_ssd_proj.astype(BF16),
        "w_out": w_out.astype(BF16),
        "w_up": w_up.astype(BF16),
        "w_down": w_down.astype(BF16),
    }


def _prep_layer(wt_dt, pool_scale, conv_w, conv_b, dt_bias, a_log, d_skip, ssd_norm, g_mix_pre, g_mix_post,
                g_mlp_pre, g_mlp_post):
    w_dt = wt_dt.reshape(N_GROUPS, HEADS_PER_GROUP, D_MODEL)
    w_dt = jnp.pad(w_dt, ((0, 0), (0, HEAD_LANE_STRIDE - HEADS_PER_GROUP), (0, 0))).reshape(DT_LANES, D_MODEL)
    row = lambda v: v.reshape(1, -1)
    return {
        "w_dt": w_dt,
        "dt_bias": _heads_to_lanes(row(dt_bias)),
        "pool_scale": row(pool_scale),
        "conv_w": conv_w,
        "conv_b": row(conv_b),
        "a_log": _heads_to_lanes(row(a_log)),
        "d_skip": row(jnp.repeat(d_skip, HEAD_DIM)),
        "ssd_norm": row(ssd_norm),
        "g_mix_pre": row(g_mix_pre),
        "g_mix_post": row(g_mix_post),
        "g_mlp_pre": row(g_mlp_pre),
        "g_mlp_post": row(g_mlp_post),
    }


@jax.jit
def kernel(x_prompt, x_sample, state_pool, state_conv, state_ssm, w_in, pool_w, pool_scale, conv_w, conv_b,
           dt_bias, a_log, d_skip, ssd_norm, w_pool_proj, w_ssd_proj, w_out, g_mix_pre, g_mix_post,
           w_up, w_down, g_mlp_pre, g_mlp_post):
    bp, lp, _ = x_prompt.shape
    bs, ls, _ = x_sample.shape
    depth = w_in.shape[0]
    past_len = 1024
    zero_pool = jnp.zeros((bp, POOL_KEEP, D_MODEL), F32)
    zero_conv = jnp.zeros((bp, CONV_W - 1, D_CONV), F32)
    zero_ssm = jnp.zeros((1, bp, N_HEADS, HEAD_DIM, D_STATE), F32)
    yp = x_prompt.reshape(bp * lp, D_MODEL)
    ys = x_sample.reshape(bs * ls, D_MODEL)
    big = _prep_big(w_in, pool_w, w_pool_proj, w_ssd_proj, w_out, w_up, w_down)
    outs = [[] for _ in range(6)]
    for li in range(depth):
        wt_dt = lax.slice(big["w_in"], (li, SRC_COL_DT, 0), (li + 1, SRC_COL_GATE, D_MODEL))[0]
        p = _prep_layer(wt_dt, pool_scale[li], conv_w[li], conv_b[li], dt_bias[li], a_log[li], d_skip[li],
                        ssd_norm[li], g_mix_pre[li], g_mix_post[li], g_mlp_pre[li], g_mlp_post[li])
        yp, pool_p, conv_p, ssm_p = _layer_group(yp, zero_pool, zero_conv, zero_ssm, 0, p, big,
                                                 li=li, ls=0, b=bp, l=lp)
        ys, pool_s, conv_s, ssm_s = _layer_group(ys, state_pool[li], state_conv[li], state_ssm, past_len, p, big,
                                                 li=li, ls=li, b=bs, l=ls)
        for lst, v in zip(outs, (pool_p, conv_p, ssm_p, pool_s, conv_s, ssm_s)):
            lst.append(v)
    return (yp.reshape(bp, lp, D_MODEL), ys.reshape(bs, ls, D_MODEL)) + tuple(jnp.stack(v) for v in outs)
```

```python
import functools

import jax
import jax.numpy as jnp
from jax import lax
from jax.experimental import pallas as pl
from jax.experimental.pallas import tpu as pltpu

F32 = jnp.float32
BF16 = jnp.bfloat16

D_MODEL = 2048
EPS = 1e-6
POOL_WINDOWS = (2, 4, 8, 16)
POOL_GROUP = D_MODEL // len(POOL_WINDOWS)
POOL_KEEP = max(POOL_WINDOWS) - 1
POOL_HALO = 16
D_INNER = 2 * D_MODEL
HEAD_DIM = 64
N_HEADS = D_INNER // HEAD_DIM
N_GROUPS = 8
HEADS_PER_GROUP = N_HEADS // N_GROUPS
GROUP_W = D_INNER // N_GROUPS
D_STATE = 128
CONV_W = 4
D_CONV = D_INNER + 2 * N_GROUPS * D_STATE
SSD_Q = 128
GROUP_SLOTS = 8
CONV_HALO = 8
D_FF = 4 * D_MODEL
COL_XBC = 0
COL_POOL = COL_XBC + D_CONV
COL_Z = COL_POOL + D_MODEL
COL_G0 = COL_Z + D_INNER
COL_G1 = COL_G0 + D_MODEL
P_COLS = COL_G1 + D_MODEL
SRC_COL_POOL = 0
SRC_COL_XBC = SRC_COL_POOL + D_MODEL + D_INNER
SRC_COL_DT = SRC_COL_XBC + D_CONV
SRC_COL_GATE = SRC_COL_DT + N_HEADS
IN_COLS = SRC_COL_GATE + 2 * D_MODEL
DT_LANES = 128
HEAD_LANE_STRIDE = DT_LANES // N_GROUPS
SEL_ROW0 = DT_LANES - HEAD_LANE_STRIDE
SEL_ROWS = SEL_ROW0 + DT_LANES
SEL_COLS = (HEADS_PER_GROUP + HEADS_PER_GROUP // 2) * 128
NEG_BIG = -1e30
LOG2E = 1.4426950408889634
LANE = 128
VMEM_LIMIT = 56 * 1024 * 1024


def _params(n_axes):
    return pltpu.CompilerParams(dimension_semantics=("arbitrary",) * n_axes, vmem_limit_bytes=VMEM_LIMIT)


def _sigmoid(v):
    return 1.0 / (1.0 + jnp.exp2(v * (-LOG2E)))


def _rms_scale(v, g):
    ms = jnp.mean(v * v, axis=-1, keepdims=True)
    return v * lax.rsqrt(ms + EPS) * g


def _dot(a, b):
    return jnp.dot(a, b, preferred_element_type=F32)


def _dot_nt(a, b):
    return lax.dot_general(a, b, (((1,), (1,)), ((), ())), preferred_element_type=F32)


def _in_proj_kernel(x_ref, g_ref, w_ref, wdt_ref, dtb_ref, p_ref, dt_ref, u_ref, *, tn):
    j = pl.program_id(1)

    @pl.when(j == 0)
    def _():
        u = _rms_scale(x_ref[...], g_ref[...]).astype(BF16)
        u_ref[...] = u
        raw = _dot_nt(u, wdt_ref[...]) + dtb_ref[...]
        dt_ref[...] = jnp.maximum(raw, 0.0) + jnp.log1p(jnp.exp(-jnp.abs(raw)))

    col = j * tn

    @pl.when(col < COL_Z)
    def _():
        p_ref[...] = _dot_nt(u_ref[...], w_ref[0]).astype(BF16)

    @pl.when(col >= COL_Z)
    def _():
        acc = _dot_nt(u_ref[...], w_ref[0])
        sig = _sigmoid(acc)
        p_ref[...] = jnp.where(col >= COL_G0, sig, acc * sig).astype(BF16)


def _in_proj(x, g, wt, wdt, dtb, *, li, tm, tn):
    m = x.shape[0]
    n_x = D_CONV // tn
    n_plain = n_x + (D_MODEL + D_INNER) // tn

    unit = N_HEADS

    def w_rows(i, j):
        plain = jnp.where(j < n_x, (SRC_COL_XBC + j * tn) // unit, (SRC_COL_POOL + (j - n_x) * tn) // unit)
        return (li, jnp.where(j < n_plain, plain, (SRC_COL_GATE + (j - n_plain) * tn) // unit) * unit, 0)

    return pl.pallas_call(
        functools.partial(_in_proj_kernel, tn=tn),
        grid=(m // tm, P_COLS // tn),
        in_specs=[
            pl.BlockSpec((tm, D_MODEL), lambda i, j: (i, 0), pipeline_mode=pl.Buffered(1)),
            pl.BlockSpec((1, D_MODEL), lambda i, j: (0, 0)),
            pl.BlockSpec((pl.Element(1), pl.Element(tn), pl.Element(D_MODEL)), w_rows),
            pl.BlockSpec((DT_LANES, D_MODEL), lambda i, j: (0, 0)),
            pl.BlockSpec((1, DT_LANES), lambda i, j: (0, 0)),
        ],
        out_specs=[
            pl.BlockSpec((tm, tn), lambda i, j: (i, j)),
            pl.BlockSpec((tm, DT_LANES), lambda i, j: (i, 0)),
        ],
        out_shape=[jax.ShapeDtypeStruct((m, P_COLS), BF16), jax.ShapeDtypeStruct((m, DT_LANES), F32)],
        scratch_shapes=[pltpu.VMEM((tm, D_MODEL), BF16)],
        compiler_params=_params(2),
        name="in_proj",
    )(x, g, wt, wdt, dtb)


def _pool_kernel(p_ref, g0_ref, sth_ref, stl_ref, pw_ref, ps_ref, wpp_ref, o_ref, hh_ref, hl_ref, a_ref,
                 *, lt, pos0):
    ti = pl.program_id(1)

    @pl.when(ti == 0)
    def _():
        hh_ref[...] = sth_ref[0]
        hl_ref[...] = stl_ref[0]

    cur = p_ref[...]
    diff = lax.broadcasted_iota(jnp.int32, (lt, lt), 0) - lax.broadcasted_iota(jnp.int32, (lt, lt), 1)
    dist = (lax.broadcasted_iota(jnp.int32, (lt, POOL_HALO), 0) + POOL_HALO
            - lax.broadcasted_iota(jnp.int32, (lt, POOL_HALO), 1))
    pos = pos0 + ti * lt + lax.broadcasted_iota(jnp.int32, (lt, 1), 0)
    for g, w in enumerate(POOL_WINDOWS):
        cs = slice(g * POOL_GROUP, (g + 1) * POOL_GROUP)
        band = jnp.where(diff >= 0, jnp.where(diff < w, 1.0, 0.0), 0.0).astype(BF16)
        hband = jnp.where(dist < w, 1.0, 0.0).astype(BF16)
        xg = cur[:, cs]
        halo = jnp.concatenate([hh_ref[:, cs], hl_ref[:, cs]], axis=0)
        win = _dot(band, xg) + _dot(jnp.concatenate([hband, hband], axis=1), halo)
        cnt = jnp.minimum(w, pos + 1).astype(F32)
        d = win * (1.0 / cnt) - xg.astype(F32)
        a_ref[:, cs] = (_dot(d.astype(BF16), pw_ref[g]) * ps_ref[:, cs]).astype(BF16)
    o_ref[...] = (_dot(a_ref[...], wpp_ref[...]) * g0_ref[...].astype(F32)).astype(BF16)
    hh_ref[...] = cur[lt - POOL_HALO:, :]
    hl_ref[...] = jnp.zeros_like(hl_ref)


def _pool_branch(pall, st_hi, st_lo, pool_w, pool_scale, wpp, *, li, b, l, lt, pos0):
    nt = l // lt
    m = b * l
    return pl.pallas_call(
        functools.partial(_pool_kernel, lt=lt, pos0=pos0),
        grid=(b, nt),
        in_specs=[
            pl.BlockSpec((lt, D_MODEL), lambda bi, ti: (bi * nt + ti, COL_POOL // D_MODEL)),
            pl.BlockSpec((lt, D_MODEL), lambda bi, ti: (bi * nt + ti, COL_G0 // D_MODEL)),
            pl.BlockSpec((1, POOL_HALO, D_MODEL), lambda bi, ti: (bi, 0, 0)),
            pl.BlockSpec((1, POOL_HALO, D_MODEL), lambda bi, ti: (bi, 0, 0)),
            pl.BlockSpec((None, len(POOL_WINDOWS), POOL_GROUP, POOL_GROUP), lambda bi, ti: (li, 0, 0, 0)),
            pl.BlockSpec((1, D_MODEL), lambda bi, ti: (0, 0)),
            pl.BlockSpec((None, D_MODEL, D_MODEL), lambda bi, ti: (li, 0, 0)),
        ],
        out_specs=pl.BlockSpec((lt, D_MODEL), lambda bi, ti: (bi * nt + ti, 0)),
        out_shape=jax.ShapeDtypeStruct((m, D_MODEL), BF16),
        scratch_shapes=[
            pltpu.VMEM((POOL_HALO, D_MODEL), BF16),
            pltpu.VMEM((POOL_HALO, D_MODEL), BF16),
            pltpu.VMEM((lt, D_MODEL), BF16),
        ],
        compiler_params=_params(2),
        name="pool_branch",
    )(pall, pall, st_hi, st_lo, pool_w, pool_scale, wpp)


def _split3(v):
    hi = v.astype(BF16)
    r1 = v - hi.astype(F32)
    mid = r1.astype(BF16)
    lo = (r1 - mid.astype(F32)).astype(BF16)
    return hi, mid, lo


def _ssd_kernel(xbc_ref, zs_ref, dt_ref, stc_ref, sts_ref, cw_ref, cb_ref, alog_ref, dsk_ref, nw_ref, sel_ref,
                o_ref, ns_ref, xc_ref, xa_ref, ht_ref, dtt_ref, cumt_ref, colb2_ref, yst2_ref, y2_ref, *, lv, nt):
    q = SSD_Q
    ti = pl.program_id(1)

    @pl.when(ti == 0)
    def _():
        xc_ref[0:CONV_HALO, :] = stc_ref[0]
        for g in range(N_GROUPS):
            sg = sts_ref[0, g * HEADS_PER_GROUP:(g + 1) * HEADS_PER_GROUP]
            ht_ref[g] = sg.reshape(GROUP_W, D_STATE).T

    if lv < q:
        xa_ref[lv:q, :] = jnp.zeros((q - lv, D_CONV), F32)
    cwid = 256

    def conv_body(c, carry):
        cs = pl.ds(pl.multiple_of(c * cwid, cwid), cwid)
        xc_ref[CONV_HALO:CONV_HALO + lv, cs] = xbc_ref[:, cs].astype(F32)
        full = xc_ref[0:CONV_HALO + lv, cs]
        acc = full[CONV_HALO:] * cw_ref[CONV_W - 1:CONV_W, cs] + cb_ref[:, cs]
        for k in range(1, CONV_W):
            tap = CONV_W - 1 - k
            acc = acc + pltpu.roll(full, k, axis=0)[CONV_HALO:] * cw_ref[tap:tap + 1, cs]
        xa_ref[0:lv, cs] = acc * _sigmoid(acc)
        return carry

    lax.fori_loop(0, D_CONV // cwid, conv_body, 0)
    xc_ref[0:CONV_HALO, :] = xc_ref[lv:lv + CONV_HALO, :]

    if lv < q:
        dt_tm = jnp.concatenate([dt_ref[...], jnp.zeros((q - lv, DT_LANES), F32)], axis=0)
    else:
        dt_tm = dt_ref[...]
    a2 = dt_tm * (-LOG2E * jnp.exp(alog_ref[...]))
    row_i = lax.broadcasted_iota(jnp.int32, (q, q), 0)
    col_i = lax.broadcasted_iota(jnp.int32, (q, q), 1)
    causal = row_i >= col_i
    lower = jnp.where(causal, 1.0, 0.0).astype(BF16)
    hi, mid, lo = _split3(a2)
    cum = _dot(lower, hi) + _dot(lower, mid) + _dot(lower, lo)
    c2 = jnp.concatenate(_split3(cum)[0:2], axis=1)
    cumt_ref[...] = cum.T
    dtt_ref[...] = dt_tm.T
    lane_row = lax.broadcasted_iota(jnp.int32, (1, LANE), 1)
    half0 = jnp.where(lane_row < HEAD_DIM, 1.0, 0.0).astype(BF16)
    half1 = jnp.where(lane_row >= HEAD_DIM, 1.0, 0.0).astype(BF16)

    def one_group(g, slot):
        colb_ref, yst_ref, y_ref = colb2_ref.at[slot], yst2_ref.at[slot], y2_ref.at[slot]
        go = g * GROUP_W
        bg = xa_ref[:, pl.ds(pl.multiple_of(D_INNER + g * D_STATE, D_STATE), D_STATE)]
        cg = xa_ref[:, pl.ds(pl.multiple_of(D_INNER + N_GROUPS * D_STATE + g * D_STATE, D_STATE), D_STATE)]
        cg16 = cg.astype(BF16)
        cb = lax.dot_general(cg16, bg.astype(BF16), (((1,), (1,)), ((), ())), preferred_element_type=F32)
        bt = bg.T
        selg = sel_ref[pl.ds(pl.multiple_of(SEL_ROW0 - HEAD_LANE_STRIDE * g, HEAD_LANE_STRIDE), DT_LANES), :]
        colb_ref[...] = _dot(c2, jnp.concatenate([selg, selg], axis=0))
        yst_ref[...] = _dot(cg16, ht_ref[g].astype(BF16))
        for j in range(HEADS_PER_GROUP // 2):
            pair = slice(j * LANE, (j + 1) * LANE)
            mixes = []
            bws = []
            lasts = []
            for e in range(2):
                r = 2 * j + e
                h = g * HEAD_LANE_STRIDE + r
                cum_row = cumt_ref[pl.ds(h, 1), :]
                last = cum_row[:, q - 1:q]
                src_row = cum_row - jnp.log2(dtt_ref[pl.ds(h, 1), :])
                cum_col = colb_ref[:, r * LANE:(r + 1) * LANE]
                mixes.append((cb * jnp.exp2(jnp.where(causal, cum_col - src_row, NEG_BIG))).astype(BF16))
                bws.append((bt * jnp.exp2(last - src_row)).astype(BF16))
                lasts.append(jnp.exp2(last))
            xs = xa_ref[:, pl.ds(pl.multiple_of(go + j * LANE, LANE), LANE)]
            xs16 = xs.astype(BF16)
            rhs = jnp.concatenate([xs16 * half0, xs16 * half1], axis=0)
            dsk = dsk_ref[:, pl.ds(pl.multiple_of(go + j * LANE, LANE), LANE)]
            cum_pair = colb_ref[:, (HEADS_PER_GROUP + j) * LANE:(HEADS_PER_GROUP + j + 1) * LANE]
            y_ref[:, pair] = (_dot(jnp.concatenate(mixes, axis=1), rhs)
                              + yst_ref[:, pair] * jnp.exp2(cum_pair) + dsk * xs)
            keep = jnp.where(lane_row < HEAD_DIM, lasts[0], lasts[1])
            ht_ref[g, :, pair] = ht_ref[g, :, pair] * keep + _dot(jnp.concatenate(bws, axis=1), rhs)
        gs = pl.ds(pl.multiple_of(go, GROUP_W), GROUP_W)
        gated = y_ref[0:lv, :] * zs_ref[:, gs].astype(F32)
        o_ref[:, gs] = _rms_scale(gated, nw_ref[:, gs]).astype(BF16)

    def group_batch_body(i, carry):
        for slot in range(GROUP_SLOTS):
            one_group(GROUP_SLOTS * i + slot, slot)
        return carry

    lax.fori_loop(0, N_GROUPS // GROUP_SLOTS, group_batch_body, 0)

    @pl.when(ti == nt - 1)
    def _():
        for g in range(N_GROUPS):
            ns_ref[0, g * HEADS_PER_GROUP:(g + 1) * HEADS_PER_GROUP] = (
                ht_ref[g].T.reshape(HEADS_PER_GROUP, HEAD_DIM, D_STATE))


def _ssd_branch(pall, dt, st_conv, st_ssm, conv_w, conv_b, alog_row, dskip_row, norm_w, sel, *, ls, b, l):
    lv = min(l, SSD_Q)
    nt = l // lv
    m = b * l
    return pl.pallas_call(
        functools.partial(_ssd_kernel, lv=lv, nt=nt),
        grid=(b, nt),
        in_specs=[
            pl.BlockSpec((lv, D_CONV), lambda bi, ti: (bi * nt + ti, COL_XBC // D_CONV)),
            pl.BlockSpec((lv, D_INNER), lambda bi, ti: (bi * nt + ti, COL_Z // D_INNER)),
            pl.BlockSpec((lv, DT_LANES), lambda bi, ti: (bi * nt + ti, 0)),
            pl.BlockSpec((1, CONV_HALO, D_CONV), lambda bi, ti: (bi, 0, 0)),
            pl.BlockSpec((None, 1, N_HEADS, HEAD_DIM, D_STATE), lambda bi, ti: (ls, bi, 0, 0, 0)),
            pl.BlockSpec((CONV_W, D_CONV), lambda bi, ti: (0, 0)),
            pl.BlockSpec((1, D_CONV), lambda bi, ti: (0, 0)),
            pl.BlockSpec((1, DT_LANES), lambda bi, ti: (0, 0)),
            pl.BlockSpec((1, D_INNER), lambda bi, ti: (0, 0)),
            pl.BlockSpec((1, D_INNER), lambda bi, ti: (0, 0)),
            pl.BlockSpec((SEL_ROWS, SEL_COLS), lambda bi, ti: (0, 0)),
        ],
        out_specs=[
            pl.BlockSpec((lv, D_INNER), lambda bi, ti: (bi * nt + ti, 0)),
            pl.BlockSpec((1, N_HEADS, HEAD_DIM, D_STATE), lambda bi, ti: (bi, 0, 0, 0)),
        ],
        out_shape=[
            jax.ShapeDtypeStruct((m, D_INNER), BF16),
            jax.ShapeDtypeStruct((b, N_HEADS, HEAD_DIM, D_STATE), F32),
        ],
        scratch_shapes=[
            pltpu.VMEM((CONV_HALO + SSD_Q, D_CONV), F32),
            pltpu.VMEM((SSD_Q, D_CONV), F32),
            pltpu.VMEM((N_GROUPS, D_STATE, GROUP_W), F32),
            pltpu.VMEM((DT_LANES, SSD_Q), F32),
            pltpu.VMEM((DT_LANES, SSD_Q), F32),
            pltpu.VMEM((GROUP_SLOTS, SSD_Q, SEL_COLS), F32),
            pltpu.VMEM((GROUP_SLOTS, SSD_Q, GROUP_W), F32),
            pltpu.VMEM((GROUP_SLOTS, SSD_Q, GROUP_W), F32),
        ],
        compiler_params=_params(2),
        name="ssd_branch",
    )(pall, pall, dt, st_conv, st_ssm, conv_w, conv_b, alog_row, dskip_row, norm_w, sel)


def _mix_out_kernel(a_ref, ws_ref, bp_ref, g1_ref, wo_ref, gp_ref, x_ref, o_ref):
    br_ssd = _dot(a_ref[...], ws_ref[...])
    merged = (bp_ref[...].astype(F32) + g1_ref[...].astype(F32) * br_ssd).astype(BF16)
    out = _dot(merged, wo_ref[...])
    o_ref[...] = x_ref[...] + _rms_scale(out, gp_ref[...])


def _mix_out(a_ssd, w_ssd, bp, pall, w_out, g_post, x, *, li, tm):
    m = x.shape[0]
    resident = pl.Buffered(1)
    return pl.pallas_call(
        _mix_out_kernel,
        grid=(m // tm,),
        in_specs=[
            pl.BlockSpec((tm, D_INNER), lambda i: (i, 0)),
            pl.BlockSpec((None, D_INNER, D_MODEL), lambda i: (li, 0, 0), pipeline_mode=resident),
            pl.BlockSpec((tm, D_MODEL), lambda i: (i, 0)),
            pl.BlockSpec((tm, D_MODEL), lambda i: (i, COL_G1 // D_MODEL)),
            pl.BlockSpec((None, D_MODEL, D_MODEL), lambda i: (li, 0, 0), pipeline_mode=resident),
            pl.BlockSpec((1, D_MODEL), lambda i: (0, 0)),
            pl.BlockSpec((tm, D_MODEL), lambda i: (i, 0)),
        ],
        out_specs=pl.BlockSpec((tm, D_MODEL), lambda i: (i, 0)),
        out_shape=jax.ShapeDtypeStruct((m, D_MODEL), F32),
        compiler_params=_params(1),
        name="mix_out",
    )(a_ssd, w_ssd, bp, pall, w_out, g_post, x)


def _mlp_kernel(x_ref, gpre_ref, wu_ref, wd_ref, gpost_ref, o_ref, u_ref, acc_ref, *, nf):
    j = pl.program_id(1)

    @pl.when(j == 0)
    def _():
        u_ref[...] = _rms_scale(x_ref[...], gpre_ref[...]).astype(BF16)
        acc_ref[...] = jnp.zeros_like(acc_ref)

    hid = jnp.maximum(_dot(u_ref[...], wu_ref[...]), 0.0)
    acc_ref[...] += _dot((hid * hid).astype(BF16), wd_ref[...])

    @pl.when(j == nf - 1)
    def _():
        o_ref[...] = x_ref[...] + _rms_scale(acc_ref[...], gpost_ref[...])


def _mlp(x, g_pre, w_up, w_down, g_post, *, li, tm, tf):
    m = x.shape[0]
    nf = D_FF // tf
    return pl.pallas_call(
        functools.partial(_mlp_kernel, nf=nf),
        grid=(m // tm, nf),
        in_specs=[
            pl.BlockSpec((tm, D_MODEL), lambda i, j: (i, 0)),
            pl.BlockSpec((1, D_MODEL), lambda i, j: (0, 0)),
            pl.BlockSpec((None, D_MODEL, tf), lambda i, j: (li, 0, j)),
            pl.BlockSpec((None, tf, D_MODEL), lambda i, j: (li, j, 0)),
            pl.BlockSpec((1, D_MODEL), lambda i, j: (0, 0)),
        ],
        out_specs=pl.BlockSpec((tm, D_MODEL), lambda i, j: (i, 0)),
        out_shape=jax.ShapeDtypeStruct((m, D_MODEL), F32),
        scratch_shapes=[pltpu.VMEM((tm, D_MODEL), BF16), pltpu.VMEM((tm, D_MODEL), F32)],
        compiler_params=_params(2),
        name="mlp",
    )(x, g_pre, w_up, w_down, g_post)


def _tiles(m):
    return min(m, 512)


def _layer_group(x, st_pool, st_conv, st_ssm, pos0, p, big, *, li, ls, b, l):
    m = b * l
    tm = _tiles(m)
    pall, dt = _in_proj(x, p["g_mix_pre"], big["w_in"], p["w_dt"], p["dt_bias"], li=li, tm=min(m, 1024), tn=2048)

    st16 = jnp.pad(st_pool, ((0, 0), (POOL_HALO - POOL_KEEP, 0), (0, 0)))
    st_hi = st16.astype(BF16)
    st_lo = (st16 - st_hi.astype(F32)).astype(BF16)
    bp = _pool_branch(pall, st_hi, st_lo, big["pool_w"], p["pool_scale"], big["w_pool_proj"],
                      li=li, b=b, l=l, lt=min(l, 256), pos0=pos0)

    stc = jnp.pad(st_conv, ((0, 0), (CONV_HALO - (CONV_W - 1), 0), (0, 0)))
    a_ssd, new_ssm = _ssd_branch(pall, dt, stc, st_ssm, p["conv_w"], p["conv_b"], p["a_log"], p["d_skip"],
                                 p["ssd_norm"], _lane_selector(), ls=ls, b=b, l=l)

    x = _mix_out(a_ssd, big["w_ssd_proj"], bp, pall, big["w_out"], p["g_mix_post"], x, li=li, tm=min(m, 256))
    x = _mlp(x, p["g_mlp_pre"], big["w_up"], big["w_down"], p["g_mlp_post"], li=li, tm=tm, tf=1024)

    p3 = pall.reshape(b, l, P_COLS)
    new_pool = p3[:, l - POOL_KEEP:, COL_POOL:COL_POOL + D_MODEL].astype(F32)
    new_conv = p3[:, l - (CONV_W - 1):, COL_XBC:COL_XBC + D_CONV].astype(F32)
    return x, new_pool, new_conv, new_ssm


def _heads_to_lanes(v):
    lead = v.shape[:-1]
    v = v.reshape(lead + (N_GROUPS, HEADS_PER_GROUP))
    v = jnp.pad(v, [(0, 0)] * len(lead) + [(0, 0), (0, HEAD_LANE_STRIDE - HEADS_PER_GROUP)])
    return v.reshape(lead + (DT_LANES,))


def _lane_selector():
    j = lax.broadcasted_iota(jnp.int32, (SEL_ROWS, SEL_COLS), 0)
    c = lax.broadcasted_iota(jnp.int32, (SEL_ROWS, SEL_COLS), 1)
    blk = c // LANE
    src = jnp.where(blk < HEADS_PER_GROUP, blk, 2 * (blk - HEADS_PER_GROUP) + (c % LANE) // HEAD_DIM)
    return jnp.where(j == SEL_ROW0 + src, 1.0, 0.0).astype(BF16)


def _prep_big(w_in, pool_w, w_pool_proj, w_ssd_proj, w_out, w_up, w_down):
    return {
        "w_in": jnp.swapaxes(w_in, 1, 2).astype(BF16),
        "pool_w": pool_w.astype(BF16),
        "w_pool_proj": w_pool_proj.astype(BF16),
        "w_ssd_proj": w_ssd_proj.astype(BF16),
        "w_out": w_out.astype(BF16),
        "w_up": w_up.astype(BF16),
        "w_down": w_down.astype(BF16),
    }


def _prep_layer(wt_dt, pool_scale, conv_w, conv_b, dt_bias, a_log, d_skip, ssd_norm, g_mix_pre, g_mix_post,
                g_mlp_pre, g_mlp_post):
    w_dt = wt_dt.reshape(N_GROUPS, HEADS_PER_GROUP, D_MODEL)
    w_dt = jnp.pad(w_dt, ((0, 0), (0, HEAD_LANE_STRIDE - HEADS_PER_GROUP), (0, 0))).reshape(DT_LANES, D_MODEL)
    row = lambda v: v.reshape(1, -1)
    return {
        "w_dt": w_dt,
        "dt_bias": _heads_to_lanes(row(dt_bias)),
        "pool_scale": row(pool_scale),
        "conv_w": conv_w,
        "conv_b": row(conv_b),
        "a_log": _heads_to_lanes(row(a_log)),
        "d_skip": row(jnp.repeat(d_skip, HEAD_DIM)),
        "ssd_norm": row(ssd_norm),
        "g_mix_pre": row(g_mix_pre),
        "g_mix_post": row(g_mix_post),
        "g_mlp_pre": row(g_mlp_pre),
        "g_mlp_post": row(g_mlp_post),
    }


@jax.jit
def kernel(x_prompt, x_sample, state_pool, state_conv, state_ssm, w_in, pool_w, pool_scale, conv_w, conv_b,
           dt_bias, a_log, d_skip, ssd_norm, w_pool_proj, w_ssd_proj, w_out, g_mix_pre, g_mix_post,
           w_up, w_down, g_mlp_pre, g_mlp_post):
    bp, lp, _ = x_prompt.shape
    bs, ls, _ = x_sample.shape
    depth = w_in.shape[0]
    past_len = 1024
    zero_pool = jnp.zeros((bp, POOL_KEEP, D_MODEL), F32)
    zero_conv = jnp.zeros((bp, CONV_W - 1, D_CONV), F32)
    zero_ssm = jnp.zeros((1, bp, N_HEADS, HEAD_DIM, D_STATE), F32)
    yp = x_prompt.reshape(bp * lp, D_MODEL)
    ys = x_sample.reshape(bs * ls, D_MODEL)
    big = _prep_big(w_in, pool_w, w_pool_proj, w_ssd_proj, w_out, w_up, w_down)
    outs = [[] for _ in range(6)]
    for li in range(depth):
        wt_dt = lax.slice(big["w_in"], (li, SRC_COL_DT, 0), (li + 1, SRC_COL_GATE, D_MODEL))[0]
        p = _prep_layer(wt_dt, pool_scale[li], conv_w[li], conv_b[li], dt_bias[li], a_log[li], d_skip[li],
                        ssd_norm[li], g_mix_pre[li], g_mix_post[li], g_mlp_pre[li], g_mlp_post[li])
        yp, pool_p, conv_p, ssm_p = _layer_group(yp, zero_pool, zero_conv, zero_ssm, 0, p, big,
                                                 li=li, ls=0, b=bp, l=lp)
        ys, pool_s, conv_s, ssm_s = _layer_group(ys, state_pool[li], state_conv[li], state_ssm, past_len, p, big,
                                                 li=li, ls=li, b=bs, l=ls)
        for lst, v in zip(outs, (pool_p, conv_p, ssm_p, pool_s, conv_s, ssm_s)):
            lst.append(v)
    return (yp.reshape(bp, lp, D_MODEL), ys.reshape(bs, ls, D_MODEL)) + tuple(jnp.stack(v) for v in outs)
```
